```python
import math
import jax
import jax.numpy as jnp
from jax import lax
import numpy as np

D_MODEL = 1024
BATCH = 16
SEQ = 256
DEPTH = 1
DEC_BATCH = 2
DEC_SEQ = 4096
PAST_LEN = 512

GRID_W = 64
N_HEADS = 8
HEAD_DIM = 64
V_DIM = 2 * HEAD_DIM
ATTN_W = N_HEADS * V_DIM
D_RNN = D_MODEL
RNN_BLOCKS = 16
RNN_BLOCK_DIM = D_RNN // RNN_BLOCKS
CONV_W = 4
LRU_C = 8.0
N_EXPERTS = 32
TOP_K = 4
D_FF = D_MODEL
SWIGLU_ALPHA = 1.702
SWIGLU_LIMIT = 7.0
ROPE_THETA = 10000.0
Q_BLOCK = 128
MOE_BLOCK = 256
EPS = 1e-6
QK_COLS = N_HEADS * 2 * HEAD_DIM
IN_COLS = 2 * QK_COLS + ATTN_W + 2 * D_RNN + 2 * D_MODEL

kernel_name = "hybrid_diffattn_rglru_moe_prefix_dit_step"


def rmsnorm(x, g):
    xf = x.astype(jnp.float32)
    y = xf * lax.rsqrt(jnp.mean(xf * xf, axis=-1, keepdims=True) + EPS)
    return (y * g.astype(jnp.float32)).astype(x.dtype)


def adaln(c, w_ada, b_ada):
    m = jax.nn.silu(c) @ w_ada + b_ada
    return jnp.split(m[..., None, :], 6, axis=-1)


def axial_rope_tables(n_tokens):
    rows = n_tokens // GRID_W
    row = jnp.repeat(jnp.arange(rows, dtype=jnp.float32), GRID_W)
    col = jnp.tile(jnp.arange(GRID_W, dtype=jnp.float32), rows)
    q4 = HEAD_DIM // 4
    inv_freq = jnp.power(ROPE_THETA, -jnp.arange(q4, dtype=jnp.float32) / q4)
    ang = jnp.concatenate([row[:, None] * inv_freq, col[:, None] * inv_freq], axis=-1)
    return jnp.cos(ang), jnp.sin(ang)


def apply_rope(x, cos, sin):
    q4 = HEAD_DIM // 4
    xr = x.astype(jnp.float32).reshape(x.shape[:-1] + (2, 2, q4))
    c = cos.reshape(cos.shape[0], 1, 1, 2, q4)
    s = sin.reshape(sin.shape[0], 1, 1, 2, q4)
    x1, x2 = xr[..., 0, :], xr[..., 1, :]
    out = jnp.stack([x1 * c - x2 * s, x2 * c + x1 * s], axis=-2)
    return out.reshape(x.shape).astype(x.dtype)


def diff_attention(q, k, v, lam, lam_init, g_sub):
    b, n_q = q.shape[:2]
    qb = q.reshape((b, n_q // Q_BLOCK, Q_BLOCK) + q.shape[2:]).swapaxes(0, 1)

    def block(qi):
        s = jnp.einsum('bqhmd,bkhmd->bhmqk', qi, k).astype(jnp.float32) * (HEAD_DIM ** -0.5)
        p = jax.nn.softmax(s, axis=-1)
        w = p[:, :, 0] - lam * p[:, :, 1]
        return jnp.einsum('bhqk,bkhv->bqhv', w.astype(v.dtype), v)

    o = lax.map(block, qb).swapaxes(0, 1).reshape(b, n_q, N_HEADS, V_DIM)
    o = rmsnorm(o, g_sub) * (1.0 - lam_init)
    return o.reshape(b, n_q, ATTN_W)


def centred_dwconv(x, w, b):
    n = x.shape[1]
    left = CONV_W // 2
    xp = jnp.pad(x, ((0, 0), (left, CONV_W - 1 - left), (0, 0)))
    return b + sum(xp[:, j:j + n] * w[j] for j in range(CONV_W))


def _linear_combine(lhs, rhs):
    a1, b1 = lhs
    a2, b2 = rhs
    return a1 * a2, a2 * b1 + b2


def rglru_scan(x, w_gate, b_gate, lam, h0, reverse):
    b, n, _ = x.shape
    xf = x.astype(jnp.float32)
    g = jnp.einsum('bsgi,kgij->kbsgj', xf.reshape(b, n, RNN_BLOCKS, RNN_BLOCK_DIM),
                   w_gate.astype(jnp.float32)).reshape(2, b, n, D_RNN)
    g = g + b_gate.astype(jnp.float32)[:, None, None, :]
    r = jax.nn.sigmoid(g[0])
    i = jax.nn.sigmoid(g[1])
    log_a = -LRU_C * r * jax.nn.softplus(-lam.astype(jnp.float32))
    a = jnp.exp(log_a)
    u = jnp.sqrt(-jnp.expm1(2.0 * log_a)) * (i * xf)
    first = n - 1 if reverse else 0
    u = u.at[:, first].add(a[:, first] * h0.astype(jnp.float32))
    _, h = lax.associative_scan(_linear_combine, (a, u), axis=1, reverse=reverse)
    return h


def mixer_inputs(x, shift, scale, g_pre, w_in_l):
    h = rmsnorm(x, g_pre) * (1 + scale) + shift
    b, n, _ = h.shape
    splits = [QK_COLS, 2 * QK_COLS, 2 * QK_COLS + ATTN_W, 2 * QK_COLS + ATTN_W + D_RNN,
              2 * QK_COLS + ATTN_W + 2 * D_RNN, 2 * QK_COLS + ATTN_W + 2 * D_RNN + D_MODEL]
    q, k, v, xr, xg, ga, gr = jnp.split(h @ w_in_l, splits, axis=-1)
    q = q.reshape(b, n, N_HEADS, 2, HEAD_DIM)
    k = k.reshape(b, n, N_HEADS, 2, HEAD_DIM)
    v = v.reshape(b, n, N_HEADS, V_DIM)
    return q, k, v, xr, xg, ga, gr


def bidir_rglru(xr, w_conv_l, b_conv_l, w_gate_l, b_gate_l, lam_l, h0):
    xc = centred_dwconv(xr, w_conv_l, b_conv_l)
    hf = rglru_scan(xc, w_gate_l[0], b_gate_l[0], lam_l[0], h0[:, 0], reverse=False)
    hb = rglru_scan(xc, w_gate_l[1], b_gate_l[1], lam_l[1], h0[:, 1], reverse=True)
    return hf, hb


def merge_branches(o_attn, o_rec, ga, gr, w_attn_proj_l, w_rec_proj_l, w_out_l):
    y = jax.nn.sigmoid(ga) * (o_attn @ w_attn_proj_l) + jax.nn.sigmoid(gr) * (o_rec @ w_rec_proj_l)
    return y @ w_out_l


def moe_ffn(h, w_router, b_router, w_up, b_up, w_down, b_down):
    shp = h.shape
    x = h.reshape(-1, D_MODEL)
    t = x.shape[0]
    logits = (x @ w_router + b_router).astype(jnp.float32)
    top_v, top_i = lax.top_k(logits, TOP_K)
    gates = jax.nn.softmax(top_v, axis=-1)
    n_assign = t * TOP_K
    flat_e = top_i.reshape(-1)
    order = jnp.argsort(flat_e)
    sorted_e = flat_e[order]
    tok = order // TOP_K
    counts = jnp.zeros((N_EXPERTS,), jnp.int32).at[flat_e].add(1)
    padded = (counts + MOE_BLOCK - 1) // MOE_BLOCK * MOE_BLOCK
    pad_end = jnp.cumsum(padded)
    pad_start = pad_end - padded
    start = jnp.cumsum(counts) - counts
    dest = pad_start[sorted_e] + jnp.arange(n_assign, dtype=jnp.int32) - start[sorted_e]
    n_blocks = n_assign // MOE_BLOCK + N_EXPERTS
    slot_tok = jnp.full((n_blocks * MOE_BLOCK,), t, jnp.int32).at[dest].set(tok)
    x_pad = jnp.concatenate([x, jnp.zeros((1, D_MODEL), x.dtype)], axis=0)
    xs = x_pad[slot_tok].reshape(n_blocks, MOE_BLOCK, D_MODEL)
    blk_start = jnp.arange(n_blocks, dtype=jnp.int32) * MOE_BLOCK
    blk_e = jnp.minimum(jnp.sum(pad_end[None, :] <= blk_start[:, None], axis=1), N_EXPERTS - 1)

    def expert_block(args):
        xb, e = args
        hu = xb @ w_up[e] + b_up[e]
        glu = jnp.minimum(hu[:, ::2], SWIGLU_LIMIT)
        lin = jnp.clip(hu[:, 1::2], -SWIGLU_LIMIT, SWIGLU_LIMIT)
        act = (lin + 1) * glu * jax.nn.sigmoid(SWIGLU_ALPHA * glu)
        return act @ w_down[e] + b_down[e]

    ys = lax.map(expert_block, (xs, blk_e)).reshape(-1, D_MODEL)
    contrib = ys[dest] * gates.reshape(-1)[order][:, None].astype(ys.dtype)
    out = jax.ops.segment_sum(contrib, tok, num_segments=t)
    return out.reshape(shp)


def ffn_sublayer(x, shift, scale, gate, g_pre, g_post, w_router, b_router, w_up, b_up, w_down, b_down):
    h = rmsnorm(x, g_pre) * (1 + scale) + shift
    return x + gate * rmsnorm(moe_ffn(h, w_router, b_router, w_up, b_up, w_down, b_down), g_post)


def setup_inputs(seed: int = 0) -> dict:
    key = jax.random.key(seed)
    ks = jax.random.split(key, 32)
    f32 = jnp.float32

    def nrm(k, shape, s):
        return jax.random.normal(k, shape, f32) * s

    a0 = jax.random.uniform(ks[31], (DEPTH, 2, D_RNN), f32, 0.9, 0.999) ** (1.0 / LRU_C)
    return {
        "x_prompt": nrm(ks[0], (BATCH, SEQ, D_MODEL), 1.0),
        "x_sample": nrm(ks[1], (DEC_BATCH, DEC_SEQ, D_MODEL), 1.0),
        "cache_k": nrm(ks[2], (DEC_BATCH, DEPTH, PAST_LEN, N_HEADS, 2, HEAD_DIM), 1.0),
        "cache_v": nrm(ks[3], (DEC_BATCH, DEPTH, PAST_LEN, N_HEADS, V_DIM), 1.0),
        "state_h": nrm(ks[4], (DEC_BATCH, DEPTH, 2, D_RNN), 0.5),
        "c": nrm(ks[5], (DEC_BATCH, D_MODEL), 1.0),
        "c_ctx": nrm(ks[6], (D_MODEL,), 1.0),
        "w_ada": nrm(ks[7], (DEPTH, D_MODEL, 6 * D_MODEL), 0.5 * D_MODEL ** -0.5),
        "b_ada": nrm(ks[8], (DEPTH, 6 * D_MODEL), 0.02),
        "g_pre_mix": 1.0 + nrm(ks[9], (DEPTH, D_MODEL), 0.1),
        "g_post_mix": 1.0 + nrm(ks[10], (DEPTH, D_MODEL), 0.1),
        "g_pre_ffn": 1.0 + nrm(ks[11], (DEPTH, D_MODEL), 0.1),
        "g_post_ffn": 1.0 + nrm(ks[12], (DEPTH, D_MODEL), 0.1),
        "w_in": nrm(ks[13], (DEPTH, D_MODEL, IN_COLS), D_MODEL ** -0.5),
        "lam_q": nrm(ks[14], (DEPTH, 2, HEAD_DIM), 0.1),
        "lam_k": nrm(ks[15], (DEPTH, 2, HEAD_DIM), 0.1),
        "g_subln": 1.0 + nrm(ks[16], (DEPTH, V_DIM), 0.1),
        "w_conv": nrm(ks[17], (DEPTH, CONV_W, D_RNN), CONV_W ** -0.5),
        "b_conv": nrm(ks[18], (DEPTH, D_RNN), 0.02),
        "w_lru_gate": nrm(ks[19], (DEPTH, 2, 2, RNN_BLOCKS, RNN_BLOCK_DIM, RNN_BLOCK_DIM), RNN_BLOCK_DIM ** -0.5),
        "b_lru_gate": nrm(ks[20], (DEPTH, 2, 2, D_RNN), 0.1),
        "lru_lambda": jnp.log(a0) - jnp.log1p(-a0),
        "w_attn_proj": nrm(ks[21], (DEPTH, ATTN_W, D_MODEL), ATTN_W ** -0.5),
        "w_rec_proj": nrm(ks[22], (DEPTH, D_RNN, D_MODEL), D_RNN ** -0.5),
        "w_out": nrm(ks[23], (DEPTH, D_MODEL, D_MODEL), D_MODEL ** -0.5),
        "w_router": nrm(ks[24], (DEPTH, D_MODEL, N_EXPERTS), D_MODEL ** -0.5),
        "b_router": nrm(ks[25], (DEPTH, N_EXPERTS), 0.01),
        "w_up": nrm(ks[26], (DEPTH, N_EXPERTS, D_MODEL, 2 * D_FF), D_MODEL ** -0.5),
        "b_up": nrm(ks[27], (DEPTH, N_EXPERTS, 2 * D_FF), 0.01),
        "w_down": nrm(ks[28], (DEPTH, N_EXPERTS, D_FF, D_MODEL), D_FF ** -0.5),
        "b_down": nrm(ks[29], (DEPTH, N_EXPERTS, D_MODEL), 0.01),
    }


def reference(x_prompt, x_sample, cache_k, cache_v, state_h, c, c_ctx,
              w_ada, b_ada, g_pre_mix, g_post_mix, g_pre_ffn, g_post_ffn,
              w_in, lam_q, lam_k, g_subln, w_conv, b_conv, w_lru_gate, b_lru_gate, lru_lambda,
              w_attn_proj, w_rec_proj, w_out, w_router, b_router, w_up, b_up, w_down, b_down):
    cos, sin = axial_rope_tables(x_sample.shape[1])
    y_p = x_prompt
    y_s = x_sample
    ks, vs, hs = [], [], []
    for l in range(DEPTH):
        lam_init = 0.8 - 0.6 * math.exp(-0.3 * l)
        lq = lam_q[l].astype(jnp.float32)
        lk = lam_k[l].astype(jnp.float32)
        lam = jnp.exp(jnp.sum(lq[0] * lk[0])) - jnp.exp(jnp.sum(lq[1] * lk[1])) + lam_init

        sh1, sc1, gt1, sh2, sc2, gt2 = adaln(c_ctx, w_ada[l], b_ada[l])
        q, k, v, xr, xg, ga, gr = mixer_inputs(y_p, sh1, sc1, g_pre_mix[l], w_in[l])
        o_attn = diff_attention(q, k, v, lam, lam_init, g_subln[l])
        h0 = jnp.zeros((y_p.shape[0], 2, D_RNN), jnp.float32)
        hf, hb = bidir_rglru(xr, w_conv[l], b_conv[l], w_lru_gate[l], b_lru_gate[l], lru_lambda[l], h0)
        o_rec = (hf + hb).astype(xg.dtype) * jax.nn.gelu(xg)
        mix = merge_branches(o_attn, o_rec, ga, gr, w_attn_proj[l], w_rec_proj[l], w_out[l])
        y_p = y_p + gt1 * rmsnorm(mix, g_post_mix[l])
        y_p = ffn_sublayer(y_p, sh2, sc2, gt2, g_pre_ffn[l], g_post_ffn[l],
                           w_router[l], b_router[l], w_up[l], b_up[l], w_down[l], b_down[l])
        ks.append(k)
        vs.append(v)
        hs.append(jnp.stack([hf[:, -1], hb[:, 0]], axis=1))

        sh1, sc1, gt1, sh2, sc2, gt2 = adaln(c, w_ada[l], b_ada[l])
        q, k, v, xr, xg, ga, gr = mixer_inputs(y_s, sh1, sc1, g_pre_mix[l], w_in[l])
        q = apply_rope(q, cos, sin)
        k = apply_rope(k, cos, sin)
        k_all = jnp.concatenate([k, cache_k[:, l].astype(k.dtype)], axis=1)
        v_all = jnp.concatenate([v, cache_v[:, l].astype(v.dtype)], axis=1)
        o_attn = diff_attention(q, k_all, v_all, lam, lam_init, g_subln[l])
        hf, hb = bidir_rglru(xr, w_conv[l], b_conv[l], w_lru_gate[l], b_lru_gate[l], lru_lambda[l], state_h[:, l])
        o_rec = (hf + hb).astype(xg.dtype) * jax.nn.gelu(xg)
        mix = merge_branches(o_attn, o_rec, ga, gr, w_attn_proj[l], w_rec_proj[l], w_out[l])
        y_s = y_s + gt1 * rmsnorm(mix, g_post_mix[l])
        y_s = ffn_sublayer(y_s, sh2, sc2, gt2, g_pre_ffn[l], g_post_ffn[l],
                           w_router[l], b_router[l], w_up[l], b_up[l], w_down[l], b_down[l])

    new_cache_k = jnp.stack(ks, axis=1)
    new_cache_v = jnp.stack(vs, axis=1)
    new_state_h = jnp.stack(hs, axis=1)
    return (y_p, y_s, new_cache_k, new_cache_v, new_state_h)
```

```python
import functools
import math

import jax
import jax.numpy as jnp
from jax import lax
from jax.experimental import pallas as pl
from jax.experimental.pallas import tpu as pltpu

F32 = jnp.float32
BF16 = jnp.bfloat16

D_MODEL = 1024
N_HEADS = 8
HEAD_DIM = 64
V_DIM = 2 * HEAD_DIM
GRID_W = 64
D_RNN = D_MODEL
RNN_BLOCKS = 16
RNN_BLOCK_DIM = D_RNN // RNN_BLOCKS
CONV_W = 4
LRU_C = 8.0
N_EXPERTS = 32
TOP_K = 4
D_FF = D_MODEL
SWIGLU_ALPHA = 1.702
SWIGLU_LIMIT = 7.0
ROPE_THETA = 10000.0
MOE_BLOCK = 256
EPS = 1e-6
N_MOD = 6
IN_PARTS = 7

LANE = 128
SUBLANE = 8
VMEM_LIMIT = 56 * 1024 * 1024

TM = 256
RNN_CB = 256
RNN_TT = 256
NEG_BIG = -1e30


def _cparams(sem):
    return pltpu.CompilerParams(dimension_semantics=sem, vmem_limit_bytes=VMEM_LIMIT)


def _full(shape):
    return pl.BlockSpec(shape, lambda *_: (0,) * len(shape))


def _resident(shape):
    return pl.BlockSpec(shape, lambda *_: (0,) * len(shape), pipeline_mode=pl.Buffered(1))


def _rms(x, g):
    return x * lax.rsqrt(jnp.mean(x * x, axis=-1, keepdims=True) + EPS) * g


def _ada_kernel(c_ref, w_ref, b_ref, o_ref):
    c = c_ref[...]
    s = c * jax.nn.sigmoid(c)
    o_ref[...] = jnp.dot(s.astype(BF16), w_ref[...].astype(BF16), preferred_element_type=F32) + b_ref[...]


def _adaln(cvec, w_ada, b_ada):
    return pl.pallas_call(
        _ada_kernel,
        grid=(N_MOD,),
        in_specs=[_full((SUBLANE, D_MODEL)),
                  pl.BlockSpec((D_MODEL, D_MODEL), lambda j: (0, j)),
                  pl.BlockSpec((1, D_MODEL), lambda j: (0, j))],
        out_specs=pl.BlockSpec((SUBLANE, D_MODEL), lambda j: (0, j)),
        out_shape=jax.ShapeDtypeStruct((SUBLANE, N_MOD * D_MODEL), F32),
        compiler_params=_cparams(("arbitrary",)),
        name="adaln",
    )(cvec, w_ada, b_ada.reshape(1, -1))


def _mod_rows(mod_ref, row, part):
    return mod_ref[pl.ds(row, 1), part * D_MODEL:(part + 1) * D_MODEL]


def _inproj_kernel(*refs, rope, row0, tiles_per_row):
    if rope:
        (x_ref, mod_ref, g_ref, w_ref, cos_ref, sin_ref,
         q_ref, k_ref, v_ref, xr_ref, xg_ref, ga_ref, gr_ref) = refs
    else:
        (x_ref, mod_ref, g_ref, w_ref,
         q_ref, k_ref, v_ref, xr_ref, xg_ref, ga_ref, gr_ref, k32_ref, v32_ref) = refs
    row = row0 + pl.program_id(0) // tiles_per_row
    shift = _mod_rows(mod_ref, row, 0)
    scale = _mod_rows(mod_ref, row, 1)
    h = (_rms(x_ref[...], g_ref[...]) * (1.0 + scale) + shift).astype(BF16)

    def proj(j):
        return jnp.dot(h, w_ref[:, j * D_MODEL:(j + 1) * D_MODEL], preferred_element_type=F32)

    def rotate(x):
        lane = lax.broadcasted_iota(jnp.int32, (x.shape[0], LANE), 1)
        first = (lane & 31) < 16
        outs = []
        for c in range(D_MODEL // LANE):
            xc = x[:, c * LANE:(c + 1) * LANE]
            partner = jnp.where(first, pltpu.roll(xc, LANE - 16, 1), pltpu.roll(xc, 16, 1))
            outs.append(xc * cos_ref[...] + partner * sin_ref[...])
        return jnp.concatenate(outs, axis=1)

    q = proj(0)
    k = proj(1)
    v = proj(2)
    if rope:
        q = rotate(q)
        k = rotate(k)
    else:
        k32_ref[...] = k
        v32_ref[...] = v
    q_ref[...] = (q * (HEAD_DIM ** -0.5)).astype(BF16)
    k_ref[...] = k.astype(BF16)
    v_ref[...] = v.astype(BF16)
    xr_ref[...] = proj(3)
    xg_ref[...] = proj(4)
    ga_ref[...] = proj(5)
    gr_ref[...] = proj(6)


def _inproj(x, mod, g_pre, w_in_bf, *, row0, tokens_per_row, rope_tabs=None):
    t = x.shape[0]
    rope = rope_tabs is not None
    tile = pl.BlockSpec((TM, D_MODEL), lambda i: (i, 0))
    in_specs = [tile, _full(mod.shape), _full((1, D_MODEL)), _resident(w_in_bf.shape)]
    args = [x, mod, g_pre.reshape(1, -1), w_in_bf]
    outs = [jax.ShapeDtypeStruct((t, D_MODEL), BF16)] * 3 + [jax.ShapeDtypeStruct((t, D_MODEL), F32)] * 4
    if rope:
        n_pos = rope_tabs[0].shape[0] // TM
        tab = pl.BlockSpec((TM, LANE), lambda i: (i % n_pos, 0))
        in_specs += [tab, tab]
        args += list(rope_tabs)
    else:
        outs += [jax.ShapeDtypeStruct((t, D_MODEL), F32)] * 2
    return pl.pallas_call(
        functools.partial(_inproj_kernel, rope=rope, row0=row0, tiles_per_row=tokens_per_row // TM),
        grid=(t // TM,),
        in_specs=in_specs,
        out_specs=[tile] * len(outs),
        out_shape=outs,
        compiler_params=_cparams(("arbitrary",)),
        name="inproj_lat" if rope else "inproj_ctx",
    )(*args)


def _rope_tables(n_tokens):
    rows = n_tokens // GRID_W
    row = jnp.repeat(jnp.arange(rows, dtype=F32), GRID_W)
    col = jnp.tile(jnp.arange(GRID_W, dtype=F32), rows)
    q4 = HEAD_DIM // 4
    inv_freq = jnp.power(ROPE_THETA, -jnp.arange(q4, dtype=F32) / q4)
    ang = jnp.concatenate([row[:, None] * inv_freq, col[:, None] * inv_freq], axis=-1)
    cos, sin = jnp.cos(ang), jnp.sin(ang)
    c64 = jnp.concatenate([cos[:, :q4], cos[:, :q4], cos[:, q4:], cos[:, q4:]], axis=-1)
    s64 = jnp.concatenate([-sin[:, :q4], sin[:, :q4], -sin[:, q4:], sin[:, q4:]], axis=-1)
    return jnp.tile(c64, (1, LANE // HEAD_DIM)), jnp.tile(s64, (1, LANE // HEAD_DIM))


def _attn_kernel(lq_ref, lk_ref, gs_ref, q_ref, k_ref, v_ref, o_ref,
                 m1_ref, l1_ref, a1_ref, m2_ref, l2_ref, a2_ref, *, tk, n_chunks, lam_init):
    q = q_ref[0]
    tq = q.shape[0]
    lane = lax.broadcasted_iota(jnp.int32, (tq, LANE), 1)
    zero = jnp.zeros_like(q)
    q1 = jnp.where(lane < HEAD_DIM, q, zero)
    q2 = jnp.where(lane >= HEAD_DIM, q, zero)
    for m_ref, l_ref, a_ref in ((m1_ref, l1_ref, a1_ref), (m2_ref, l2_ref, a2_ref)):
        m_ref[...] = jnp.full(m_ref.shape, -jnp.inf, F32)
        l_ref[...] = jnp.zeros(l_ref.shape, F32)
        a_ref[...] = jnp.zeros(a_ref.shape, F32)

    def chunk(c, carry):
        start = pl.multiple_of(c * tk, tk)
        kc = k_ref[0, pl.ds(start, tk), :]
        vc = v_ref[0, pl.ds(start, tk), :]
        for qm, m_ref, l_ref, a_ref in ((q1, m1_ref, l1_ref, a1_ref), (q2, m2_ref, l2_ref, a2_ref)):
            s = lax.dot_general(qm, kc, (((1,), (1,)), ((), ())), preferred_element_type=F32)
            m_old = m_ref[...]
            m_new = jnp.maximum(m_old, jnp.max(s, axis=-1, keepdims=True))
            alpha = jnp.exp(m_old - m_new)
            p = jnp.exp(s - m_new)
            l_ref[...] = alpha * l_ref[...] + jnp.sum(p, axis=-1, keepdims=True)
            a_ref[...] = alpha * a_ref[...] + jnp.dot(p.astype(BF16), vc, preferred_element_type=F32)
            m_ref[...] = m_new
        return carry

    lax.fori_loop(0, n_chunks, chunk, 0)

    e = jnp.exp(jnp.sum(lq_ref[...] * lk_ref[...], axis=-1, keepdims=True))
    lam = e[0:1, :] - e[1:2, :] + lam_init
    o = a1_ref[...] / l1_ref[...] - lam * (a2_ref[...] / l2_ref[...])
    o_ref[0] = (_rms(o, gs_ref[...]) * (1.0 - lam_init)).astype(o_ref.dtype)


def _attention(q, k, v, lam_q, lam_k, g_sub, *, lam_init, tq, tk):
    b, nq, _ = q.shape
    nk = k.shape[1]
    kern = functools.partial(_attn_kernel, tk=tk, n_chunks=nk // tk, lam_init=lam_init)
    stat = pltpu.VMEM((tq, 1), F32)
    acc = pltpu.VMEM((tq, LANE), F32)
    return pl.pallas_call(
        kern,
        grid=(b, N_HEADS, nq // tq),
        in_specs=[_full((2, HEAD_DIM)), _full((2, HEAD_DIM)), _full((1, V_DIM)),
                  pl.BlockSpec((1, tq, LANE), lambda bi, h, qi: (bi, qi, h)),
                  pl.BlockSpec((1, nk, LANE), lambda bi, h, qi: (bi, 0, h)),
                  pl.BlockSpec((1, nk, LANE), lambda bi, h, qi: (bi, 0, h))],
        out_specs=pl.BlockSpec((1, tq, LANE), lambda bi, h, qi: (bi, qi, h)),
        out_shape=jax.ShapeDtypeStruct((b, nq, D_MODEL), BF16),
        scratch_shapes=[stat, stat, acc, stat, stat, acc],
        compiler_params=_cparams(("arbitrary", "arbitrary", "arbitrary")),
        name=f"diff_attn_{nk}",
    )(lam_q, lam_k, g_sub.reshape(1, -1), q, k, v)


def _gelu_tanh(x):
    return 0.5 * x * (1.0 + jnp.tanh(math.sqrt(2.0 / math.pi) * (x + 0.044715 * (x * x * x))))


def _rnn_kernel(xr_ref, xg_ref, h0_ref, wc_ref, bc_ref, wg_ref, bg_ref, lam_ref,
                o_ref, hT_ref, xpad_ref, hf_ref, hb_ref, *, n, tt):
    pad = SUBLANE
    cb = xr_ref.shape[2]
    n_chunks = n // tt
    groups = tt // SUBLANE
    xpad_ref[0:pad, :] = jnp.zeros((pad, cb), F32)
    xpad_ref[pad + n:pad + n + pad, :] = jnp.zeros((pad, cb), F32)
    xpad_ref[pad:pad + n, :] = xr_ref[0]
    r8 = lax.broadcasted_iota(jnp.int32, (tt, cb), 0) & (SUBLANE - 1)

    def gate_inputs(c, d):
        t0 = pl.multiple_of(c * tt, tt)
        blk = xpad_ref[pl.ds(t0, tt + 2 * pad), :]
        ext = tt + 2 * pad
        xc = (bc_ref[...]
              + wc_ref[0:1, :] * pltpu.roll(blk, 2, 0)[pad:pad + tt]
              + wc_ref[1:2, :] * pltpu.roll(blk, 1, 0)[pad:pad + tt]
              + wc_ref[2:3, :] * blk[pad:pad + tt]
              + wc_ref[3:4, :] * pltpu.roll(blk, ext - 1, 0)[pad:pad + tt])
        g = jnp.dot(xc.astype(BF16), wg_ref[d, 0], preferred_element_type=F32)
        r = jax.nn.sigmoid(g[:, :cb] + bg_ref[d, 0:1, :])
        i = jax.nn.sigmoid(g[:, cb:] + bg_ref[d, 1:2, :])
        lam = lam_ref[d:d + 1, :]
        softplus = jnp.maximum(-lam, 0.0) + jnp.log1p(jnp.exp(-jnp.abs(lam)))
        log_a = (-LRU_C) * r * softplus
        a = jnp.exp(log_a)
        u = jnp.sqrt(-jnp.tanh(log_a) * (1.0 + a * a)) * (i * xc)
        return t0, a, u

    def fwd_chunk(c, hc):
        t0, a, u = gate_inputs(c, 0)
        for d in (1, 2, 4):
            keep = r8 >= d
            u = jnp.where(keep, a * pltpu.roll(u, d, 0) + u, u)
            a = jnp.where(keep, a * pltpu.roll(a, d, 0), a)
        for g in range(groups):
            hg = a[g * SUBLANE:(g + 1) * SUBLANE] * hc + u[g * SUBLANE:(g + 1) * SUBLANE]
            hf_ref[pl.ds(t0 + g * SUBLANE, SUBLANE), :] = hg
            hc = hg[SUBLANE - 1:SUBLANE]
        return hc

    def bwd_chunk(j, hc):
        c = n_chunks - 1 - j
        t0, a, u = gate_inputs(c, 1)
        for d in (1, 2, 4):
            keep = r8 < SUBLANE - d
            u = jnp.where(keep, a * pltpu.roll(u, tt - d, 0) + u, u)
            a = jnp.where(keep, a * pltpu.roll(a, tt - d, 0), a)
        for g in reversed(range(groups)):
            hg = a[g * SUBLANE:(g + 1) * SUBLANE] * hc + u[g * SUBLANE:(g + 1) * SUBLANE]
            hb_ref[g * SUBLANE:(g + 1) * SUBLANE, :] = hg
            hc = hg[0:1]
        hsum = hf_ref[pl.ds(t0, tt), :] + hb_ref[...]
        o_ref[0, pl.ds(t0, tt), :] = (hsum * _gelu_tanh(xg_ref[0, pl.ds(t0, tt), :])).astype(o_ref.dtype)
        return hc

    h_fwd = lax.fori_loop(0, n_chunks, fwd_chunk, h0_ref[0, 0:1, :])
    h_bwd = lax.fori_loop(0, n_chunks, bwd_chunk, h0_ref[0, 1:2, :])
    hT_ref[0, 0:1, :] = h_fwd
    hT_ref[0, 1:2, :] = h_bwd


def _rglru(xr, xg, h0, w_conv, b_conv, wg_bd, b_gate, lam):
    b, n, _ = xr.shape
    tt = min(RNN_TT, n)
    n_cb = D_RNN // RNN_CB
    slab = pl.BlockSpec((1, n, RNN_CB), lambda bi, ci: (bi, 0, ci))
    state = pl.BlockSpec((1, 2, RNN_CB), lambda bi, ci: (bi, 0, ci))
    return pl.pallas_call(
        functools.partial(_rnn_kernel, n=n, tt=tt),
        grid=(b, n_cb),
        in_specs=[slab, slab, state,
                  pl.BlockSpec((CONV_W, RNN_CB), lambda bi, ci: (0, ci)),
                  pl.BlockSpec((1, RNN_CB), lambda bi, ci: (0, ci)),
                  pl.BlockSpec((2, 1, RNN_CB, 2 * RNN_CB), lambda bi, ci: (0, ci, 0, 0)),
                  pl.BlockSpec((2, 2, RNN_CB), lambda bi, ci: (0, 0, ci)),
                  pl.BlockSpec((2, RNN_CB), lambda bi, ci: (0, ci))],
        out_specs=[slab, state],
        out_shape=[jax.ShapeDtypeStruct((b, n, D_RNN), BF16), jax.ShapeDtypeStruct((b, 2, D_RNN), F32)],
        scratch_shapes=[pltpu.VMEM((n + 2 * SUBLANE, RNN_CB), F32), pltpu.VMEM((n, RNN_CB), F32),
                        pltpu.VMEM((tt, RNN_CB), F32)],
        compiler_params=_cparams(("arbitrary", "arbitrary")),
        name=f"rglru_{n}",
    )(xr, xg, h0, w_conv, b_conv.reshape(1, -1), wg_bd, b_gate, lam)


def _gate_weights(w_gate):
    per = RNN_CB // RNN_BLOCK_DIM
    n_cb = D_RNN // RNN_CB
    w = w_gate.reshape(2, 2, n_cb, per, RNN_BLOCK_DIM, RNN_BLOCK_DIM)
    eye = jnp.eye(per, dtype=w.dtype)
    bd = jnp.einsum('dkcpij,pq->dkcpiqj', w, eye).reshape(2, 2, n_cb, RNN_CB, RNN_CB)
    return jnp.concatenate([bd[:, 0], bd[:, 1]], axis=-1).astype(BF16)


def _merge_kernel(oa_ref, or_ref, ga_ref, gr_ref, x_ref, mod_ref, base_ref, gpost_ref, gpre_ref,
                  wa_ref, wr_ref, wo_ref, wrh_ref, wrl_ref, br_ref,
                  y_ref, h_ref, info_ref, gate_ref, cnt_ref, run_ref, *, row0, tiles_per_row):
    i = pl.program_id(0)
    row = row0 + i // tiles_per_row
    ya = jnp.dot(oa_ref[...], wa_ref[...], preferred_element_type=F32)
    yr = jnp.dot(or_ref[...], wr_ref[...], preferred_element_type=F32)
    y = jax.nn.sigmoid(ga_ref[...]) * ya + jax.nn.sigmoid(gr_ref[...]) * yr
    mix = jnp.dot(y.astype(BF16), wo_ref[...], preferred_element_type=F32)
    y1 = x_ref[...] + _mod_rows(mod_ref, row, 2) * _rms(mix, gpost_ref[...])
    y_ref[...] = y1
    h = _rms(y1, gpre_ref[...]) * (1.0 + _mod_rows(mod_ref, row, 4)) + _mod_rows(mod_ref, row, 3)
    h_ref[...] = h

    h_hi = h.astype(BF16)
    h_lo = (h - h_hi.astype(F32)).astype(BF16)
    logits = (jnp.dot(h_hi, wrh_ref[...], preferred_element_type=F32)
              + jnp.dot(h_lo, wrh_ref[...], preferred_element_type=F32)
              + jnp.dot(h_hi, wrl_ref[...], preferred_element_type=F32)) + br_ref[...]
    tm = logits.shape[0]
    lane = lax.broadcasted_iota(jnp.int32, (tm, LANE), 1)
    work = logits
    sel = []
    vals = []
    for _ in range(TOP_K):
        mx = jnp.max(work, axis=-1, keepdims=True)
        idx = jnp.min(jnp.where(work == mx, lane, LANE), axis=-1, keepdims=True)
        hit = lane == idx
        sel.append((idx, hit))
        vals.append(mx)
        work = jnp.where(hit, NEG_BIG * 2.0, work)
    exps = [jnp.exp(v - vals[0]) for v in vals]
    inv = 1.0 / (exps[0] + exps[1] + exps[2] + exps[3])

    @pl.when(i == 0)
    def _():
        run_ref[...] = base_ref[...]

    mask = jnp.zeros((tm, LANE), F32)
    for _, hit in sel:
        mask = mask + hit.astype(F32)
    rr = lax.broadcasted_iota(jnp.int32, (tm, tm), 0)
    cc = lax.broadcasted_iota(jnp.int32, (tm, tm), 1)
    tri = (cc < rr).astype(BF16)
    rank = jnp.dot(tri, mask.astype(BF16), preferred_element_type=F32) + run_ref[...]
    run_ref[...] = run_ref[...] + jnp.sum(mask, axis=0, keepdims=True)
    cnt_ref[...] = run_ref[...]

    info = jnp.zeros((tm, LANE), jnp.int32)
    gates = jnp.zeros((tm, LANE), F32)
    for kk, (idx, hit) in enumerate(sel):
        rk = jnp.sum(jnp.where(hit, rank, 0.0), axis=-1, keepdims=True).astype(jnp.int32)
        info = jnp.where(lane == kk, idx, info)
        info = jnp.where(lane == TOP_K + kk, rk, info)
        gates = jnp.where(lane == kk, exps[kk] * inv, gates)
    info_ref[...] = info
    gate_ref[...] = gates


def _merge(oa, orec, ga, gr, x, mod, base, g_post, g_pre, wa, wr, wo, wr_hi, wr_lo, b_router,
           *, row0, tokens_per_row):
    t = x.shape[0]
    tile = pl.BlockSpec((TM, D_MODEL), lambda i: (i, 0))
    small = pl.BlockSpec((TM, LANE), lambda i: (i, 0))
    vec = _full((1, D_MODEL))
    wspec = _resident((D_MODEL, D_MODEL))
    rspec = _resident((D_MODEL, LANE))
    return pl.pallas_call(
        functools.partial(_merge_kernel, row0=row0, tiles_per_row=tokens_per_row // TM),
        grid=(t // TM,),
        in_specs=[tile, tile, tile, tile, tile, _full(mod.shape), _full((1, LANE)), vec, vec,
                  wspec, wspec, wspec, rspec, rspec, _full((1, LANE))],
        out_specs=[tile, tile, small, small, _full((1, LANE))],
        out_shape=[jax.ShapeDtypeStruct((t, D_MODEL), F32), jax.ShapeDtypeStruct((t, D_MODEL), F32),
                   jax.ShapeDtypeStruct((t, LANE), jnp.int32), jax.ShapeDtypeStruct((t, LANE), F32),
                   jax.ShapeDtypeStruct((1, LANE), F32)],
        scratch_shapes=[pltpu.VMEM((1, LANE), F32)],
        compiler_params=_cparams(("arbitrary",)),
        name=f"merge_router_{t}",
    )(oa, orec, ga, gr, x, mod, base, g_post.reshape(1, -1), g_pre.reshape(1, -1),
      wa, wr, wo, wr_hi, wr_lo, b_router)


TD = 128


def _dispatch_kernel(dest_ref, h_ref, xs_in_ref, xs_ref, sem):
    del xs_in_ref
    base = pl.program_id(0) * TD

    def copy(j):
        tok = base + j // TOP_K
        return pltpu.make_async_copy(h_ref.at[pl.ds(tok, 1), :], xs_ref.at[pl.ds(dest_ref[j], 1), :], sem)

    def start(j, c):
        copy(j).start()
        return c

    def wait(j, c):
        copy(j).wait()
        return c

    lax.fori_loop(0, TD * TOP_K, start, 0)
    lax.fori_loop(0, TD * TOP_K, wait, 0)


def _dispatch(dest_flat, h, xs):
    t = h.shape[0]
    return pl.pallas_call(
        _dispatch_kernel,
        grid=(t // TD,),
        in_specs=[pl.BlockSpec((TD * TOP_K,), lambda i: (i,), memory_space=pltpu.SMEM),
                  pl.BlockSpec(memory_space=pl.ANY), pl.BlockSpec(memory_space=pl.ANY)],
        out_specs=pl.BlockSpec(memory_space=pl.ANY),
        out_shape=jax.ShapeDtypeStruct(xs.shape, xs.dtype),
        scratch_shapes=[pltpu.SemaphoreType.DMA],
        input_output_aliases={2: 0},
        compiler_params=_cparams(("arbitrary",)),
        name=f"moe_dispatch_{t}",
    )(dest_flat, h, xs)


def _expert_kernel(be_ref, nb_ref, xs_ref, wu_ref, bu_ref, wd_ref, bd_ref, ys_ref):
    i = pl.program_id(0)

    @pl.when(i < nb_ref[0])
    def _():
        hu = jnp.dot(xs_ref[...].astype(BF16), wu_ref[0], preferred_element_type=F32) + bu_ref[0]
        glu = jnp.minimum(hu[:, :D_FF], SWIGLU_LIMIT)
        lin = jnp.clip(hu[:, D_FF:], -SWIGLU_LIMIT, SWIGLU_LIMIT)
        act = (lin + 1.0) * glu * jax.nn.sigmoid(SWIGLU_ALPHA * glu)
        ys_ref[...] = jnp.dot(act.astype(BF16), wd_ref[0], preferred_element_type=F32) + bd_ref[0]

    @pl.when(i >= nb_ref[0])
    def _():
        ys_ref[...] = jnp.zeros(ys_ref.shape, F32)


def _experts(blk_e, n_used, xs, wu, bu, wd, bd):
    ns = xs.shape[0]
    grid_spec = pltpu.PrefetchScalarGridSpec(
        num_scalar_prefetch=2,
        grid=(ns // MOE_BLOCK,),
        in_specs=[pl.BlockSpec((MOE_BLOCK, D_MODEL), lambda i, be, nb: (i, 0)),
                  pl.BlockSpec((1, D_MODEL, 2 * D_FF), lambda i, be, nb: (be[i], 0, 0)),
                  pl.BlockSpec((1, 1, 2 * D_FF), lambda i, be, nb: (be[i], 0, 0)),
                  pl.BlockSpec((1, D_FF, D_MODEL), lambda i, be, nb: (be[i], 0, 0)),
                  pl.BlockSpec((1, 1, D_MODEL), lambda i, be, nb: (be[i], 0, 0))],
        out_specs=pl.BlockSpec((MOE_BLOCK, D_MODEL), lambda i, be, nb: (i, 0)),
    )
    return pl.pallas_call(
        _expert_kernel,
        grid_spec=grid_spec,
        out_shape=jax.ShapeDtypeStruct((ns, D_MODEL), F32),
        compiler_params=_cparams(("arbitrary",)),
        name="moe_experts",
    )(blk_e, n_used, xs, wu, bu, wd, bd)


def _combine_kernel(dest_ref, ys_ref, gate_ref, y1_ref, mod_ref, gpost_ref, o_ref, buf_ref, sem,
                    *, row0, tiles_per_row):
    row = row0 + pl.program_id(0) // tiles_per_row

    def copy(j):
        return pltpu.make_async_copy(ys_ref.at[pl.ds(dest_ref[j], 1), :],
                                     buf_ref.at[j % TOP_K, pl.ds(j // TOP_K, 1), :], sem)

    def start(j, c):
        copy(j).start()
        return c

    def wait(j, c):
        copy(j).wait()
        return c

    lax.fori_loop(0, TD * TOP_K, start, 0)
    lax.fori_loop(0, TD * TOP_K, wait, 0)
    g = gate_ref[...]
    moe = g[:, 0:1] * buf_ref[0]
    for kk in range(1, TOP_K):
        moe = moe + g[:, kk:kk + 1] * buf_ref[kk]
    o_ref[...] = y1_ref[...] + _mod_rows(mod_ref, row, 5) * _rms(moe, gpost_ref[...])


def _combine(dest_flat, ys, gates, y1, mod, g_post, *, row0, tokens_per_row):
    t = y1.shape[0]
    tile = pl.BlockSpec((TD, D_MODEL), lambda i: (i, 0))
    return pl.pallas_call(
        functools.partial(_combine_kernel, row0=row0, tiles_per_row=tokens_per_row // TD),
        grid=(t // TD,),
        in_specs=[pl.BlockSpec((TD * TOP_K,), lambda i: (i,), memory_space=pltpu.SMEM),
                  pl.BlockSpec(memory_space=pl.ANY),
                  pl.BlockSpec((TD, LANE), lambda i: (i, 0)), tile, _full(mod.shape), _full((1, D_MODEL))],
        out_specs=tile,
        out_shape=jax.ShapeDtypeStruct((t, D_MODEL), F32),
        scratch_shapes=[pltpu.VMEM((TOP_K, TD, D_MODEL), F32), pltpu.SemaphoreType.DMA],
        compiler_params=_cparams(("arbitrary",)),
        name=f"moe_combine_{t}",
    )(dest_flat, ys, gates, y1, mod, g_post.reshape(1, -1))


def kernel(x_prompt, x_sample, cache_k, cache_v, state_h, c, c_ctx, w_ada, b_ada, g_pre_mix, g_post_mix, g_pre_ffn, g_post_ffn, w_in, lam_q, lam_k, g_subln, w_conv, b_conv, w_lru_gate, b_lru_gate, lru_lambda, w_attn_proj, w_rec_proj, w_out, w_router, b_router, w_up, b_up, w_down, b_down):
    depth = w_in.shape[0]
    bp, sp, _ = x_prompt.shape
    bs, ss, _ = x_sample.shape
    tp, ts = bp * sp, bs * ss
    cos_t, sin_t = _rope_tables(ss)

    y_p = x_prompt.reshape(tp, D_MODEL)
    y_s = x_sample.reshape(ts, D_MODEL)
    ks, vs, hs = [], [], []
    for l in range(depth):
        lam_init = 0.8 - 0.6 * math.exp(-0.3 * l)
        cvec = jnp.zeros((SUBLANE, D_MODEL), F32).at[0].set(c_ctx).at[1:1 + bs].set(c)
        mod = _adaln(cvec, w_ada[l], b_ada[l])
        w_in_bf = w_in[l].astype(BF16)
        wa, wr, wo = w_attn_proj[l].astype(BF16), w_rec_proj[l].astype(BF16), w_out[l].astype(BF16)
        wg_bd = _gate_weights(w_lru_gate[l])
        w_rt = jnp.pad(w_router[l], ((0, 0), (0, LANE - N_EXPERTS)))
        wr_hi = w_rt.astype(BF16)
        wr_lo = (w_rt - wr_hi.astype(F32)).astype(BF16)
        b_rt = jnp.pad(b_router[l], (0, LANE - N_EXPERTS), constant_values=NEG_BIG).reshape(1, LANE)
        wu = w_up[l].reshape(N_EXPERTS, D_MODEL, D_FF, 2)
        wu = jnp.concatenate([wu[..., 0], wu[..., 1]], axis=-1).astype(BF16)
        bu = b_up[l].reshape(N_EXPERTS, D_FF, 2)
        bu = jnp.concatenate([bu[..., 0], bu[..., 1]], axis=-1).reshape(N_EXPERTS, 1, 2 * D_FF)
        wd = w_down[l].astype(BF16)
        bd = b_down[l].reshape(N_EXPERTS, 1, D_MODEL)

        q, k, v, xr, xg, ga, gr, k32, v32 = _inproj(y_p, mod, g_pre_mix[l], w_in_bf, row0=0, tokens_per_row=tp)
        oa = _attention(q.reshape(bp, sp, D_MODEL), k.reshape(bp, sp, D_MODEL), v.reshape(bp, sp, D_MODEL),
                        lam_q[l], lam_k[l], g_subln[l], lam_init=lam_init, tq=sp, tk=sp)
        orec, h_t = _rglru(xr.reshape(bp, sp, D_RNN), xg.reshape(bp, sp, D_RNN),
                           jnp.zeros((bp, 2, D_RNN), F32), w_conv[l], b_conv[l], wg_bd, b_lru_gate[l], lru_lambda[l])
        ks.append(k32.reshape(bp, sp, N_HEADS, 2, HEAD_DIM))
        vs.append(v32.reshape(bp, sp, N_HEADS, V_DIM))
        hs.append(h_t)
        y1_p, h2_p, info_p, gate_p, cnt_p = _merge(
            oa.reshape(tp, D_MODEL), orec.reshape(tp, D_MODEL), ga, gr, y_p, mod, jnp.zeros((1, LANE), F32),
            g_post_mix[l], g_pre_ffn[l], wa, wr, wo, wr_hi, wr_lo, b_rt, row0=0, tokens_per_row=tp)

        q, k, v, xr, xg, ga, gr = _inproj(y_s, mod, g_pre_mix[l], w_in_bf, row0=1, tokens_per_row=ss,
                                          rope_tabs=(cos_t, sin_t))
        k_all = jnp.concatenate([k.reshape(bs, ss, D_MODEL),
                                 cache_k[:, l].reshape(bs, -1, D_MODEL).astype(BF16)], axis=1)
        v_all = jnp.concatenate([v.reshape(bs, ss, D_MODEL),
                                 cache_v[:, l].reshape(bs, -1, D_MODEL).astype(BF16)], axis=1)
        oa = _attention(q.reshape(bs, ss, D_MODEL), k_all, v_all, lam_q[l], lam_k[l], g_subln[l],
                        lam_init=lam_init, tq=256, tk=512)
        orec, _ = _rglru(xr.reshape(bs, ss, D_RNN), xg.reshape(bs, ss, D_RNN), state_h[:, l],
                         w_conv[l], b_conv[l], wg_bd, b_lru_gate[l], lru_lambda[l])
        y1_s, h2_s, info_s, gate_s, cnt_s = _merge(
            oa.reshape(ts, D_MODEL), orec.reshape(ts, D_MODEL), ga, gr, y_s, mod, cnt_p,
            g_post_mix[l], g_pre_ffn[l], wa, wr, wo, wr_hi, wr_lo, b_rt, row0=1, tokens_per_row=ss)

        counts = cnt_s[0, :N_EXPERTS].astype(jnp.int32)
        padded = (counts + MOE_BLOCK - 1) // MOE_BLOCK * MOE_BLOCK
        pad_end = jnp.cumsum(padded)
        pad_start = pad_end - padded
        n_slots = (tp + ts) * TOP_K + N_EXPERTS * MOE_BLOCK
        n_blocks = n_slots // MOE_BLOCK
        blk_start = jnp.arange(n_blocks, dtype=jnp.int32) * MOE_BLOCK
        blk_e = jnp.minimum(jnp.sum(pad_end[None, :] <= blk_start[:, None], axis=1), N_EXPERTS - 1).astype(jnp.int32)
        n_used = (pad_end[-1:] // MOE_BLOCK).astype(jnp.int32)

        def dest_of(info):
            e = info[:, :TOP_K]
            onehot = e[:, :, None] == jnp.arange(N_EXPERTS, dtype=jnp.int32)[None, None, :]
            return (jnp.sum(jnp.where(onehot, pad_start[None, None, :], 0), axis=-1)
                    + info[:, TOP_K:2 * TOP_K]).reshape(-1)

        dest_p, dest_s = dest_of(info_p), dest_of(info_s)
        xs = jnp.zeros((n_slots, D_MODEL), F32)
        xs = _dispatch(dest_p, h2_p, xs)
        xs = _dispatch(dest_s, h2_s, xs)
        ys = _experts(blk_e, n_used, xs, wu, bu, wd, bd)
        y_p = _combine(dest_p, ys, gate_p, y1_p, mod, g_post_ffn[l], row0=0, tokens_per_row=tp)
        y_s = _combine(dest_s, ys, gate_s, y1_s, mod, g_post_ffn[l], row0=1, tokens_per_row=ss)

    return (y_p.reshape(bp, sp, D_MODEL), y_s.reshape(bs, ss, D_MODEL),
            jnp.stack(ks, axis=1), jnp.stack(vs, axis=1), jnp.stack(hs, axis=1))
```

```python
import functools
import math

import jax
import jax.numpy as jnp
from jax import lax
from jax.experimental import pallas as pl
from jax.experimental.pallas import tpu as pltpu

F32 = jnp.float32
BF16 = jnp.bfloat16

D_MODEL = 1024
N_HEADS = 8
HEAD_DIM = 64
V_DIM = 2 * HEAD_DIM
GRID_W = 64
D_RNN = D_MODEL
RNN_BLOCKS = 16
RNN_BLOCK_DIM = D_RNN // RNN_BLOCKS
CONV_W = 4
LRU_C = 8.0
N_EXPERTS = 32
TOP_K = 4
D_FF = D_MODEL
SWIGLU_ALPHA = 1.702
SWIGLU_LIMIT = 7.0
ROPE_THETA = 10000.0
MOE_BLOCK = 256
EPS = 1e-6
N_MOD = 6
IN_PARTS = 7

LANE = 128
SUBLANE = 8
VMEM_LIMIT = 56 * 1024 * 1024

TM = 256
RNN_CB = 256
RNN_TT = 256
NEG_BIG = -1e30
Q_SCALE = HEAD_DIM ** -0.5 * math.log2(math.e)


def _cparams(sem):
    return pltpu.CompilerParams(dimension_semantics=sem, vmem_limit_bytes=VMEM_LIMIT)


def _full(shape):
    return pl.BlockSpec(shape, lambda *_: (0,) * len(shape))


def _resident(shape):
    return pl.BlockSpec(shape, lambda *_: (0,) * len(shape), pipeline_mode=pl.Buffered(1))


def _rms(x, g):
    return x * lax.rsqrt(jnp.mean(x * x, axis=-1, keepdims=True) + EPS) * g


def _ada_kernel(c_ref, w_ref, b_ref, o_ref):
    c = c_ref[...]
    s = c * jax.nn.sigmoid(c)
    o_ref[...] = jnp.dot(s.astype(BF16), w_ref[...].astype(BF16), preferred_element_type=F32) + b_ref[...]


def _adaln(cvec, w_ada, b_ada):
    return pl.pallas_call(
        _ada_kernel,
        grid=(N_MOD,),
        in_specs=[_full((SUBLANE, D_MODEL)),
                  pl.BlockSpec((D_MODEL, D_MODEL), lambda j: (0, j)),
                  pl.BlockSpec((1, D_MODEL), lambda j: (0, j))],
        out_specs=pl.BlockSpec((SUBLANE, D_MODEL), lambda j: (0, j)),
        out_shape=jax.ShapeDtypeStruct((SUBLANE, N_MOD * D_MODEL), F32),
        compiler_params=_cparams(("arbitrary",)),
        name="adaln",
    )(cvec, w_ada, b_ada.reshape(1, -1))


def _mod_rows(mod_ref, row, part):
    return mod_ref[pl.ds(row, 1), part * D_MODEL:(part + 1) * D_MODEL]


def _inproj_kernel(*refs, rope, row0, tiles_per_row):
    if rope:
        (x_ref, mod_ref, g_ref, w_ref, cos_ref, sin_ref,
         q_ref, k_ref, v_ref, xr_ref, xg_ref, ga_ref, gr_ref) = refs
    else:
        (x_ref, mod_ref, g_ref, w_ref,
         q_ref, k_ref, v_ref, xr_ref, xg_ref, ga_ref, gr_ref, k32_ref, v32_ref) = refs
    row = row0 + pl.program_id(0) // tiles_per_row
    shift = _mod_rows(mod_ref, row, 0)
    scale = _mod_rows(mod_ref, row, 1)
    h = (_rms(x_ref[...], g_ref[...]) * (1.0 + scale) + shift).astype(BF16)

    def proj(j):
        return jnp.dot(h, w_ref[:, j * D_MODEL:(j + 1) * D_MODEL], preferred_element_type=F32)

    def rotate(x):
        lane = lax.broadcasted_iota(jnp.int32, (x.shape[0], LANE), 1)
        first = (lane & 31) < 16
        outs = []
        for c in range(D_MODEL // LANE):
            xc = x[:, c * LANE:(c + 1) * LANE]
            partner = jnp.where(first, pltpu.roll(xc, LANE - 16, 1), pltpu.roll(xc, 16, 1))
            outs.append(xc * cos_ref[...] + partner * sin_ref[...])
        return jnp.concatenate(outs, axis=1)

    q = proj(0)
    k = proj(1)
    v = proj(2)
    if rope:
        q = rotate(q)
        k = rotate(k)
    else:
        k32_ref[...] = k
        v32_ref[...] = v
    q_ref[...] = (q * Q_SCALE).astype(BF16)
    k_ref[...] = k.astype(BF16)
    v_ref[...] = v.astype(BF16)
    xr_ref[...] = proj(3)
    xg_ref[...] = proj(4)
    ga_ref[...] = proj(5)
    gr_ref[...] = proj(6)


def _inproj(x, mod, g_pre, w_in_bf, *, row0, tokens_per_row, rope_tabs=None):
    t = x.shape[0]
    rope = rope_tabs is not None
    tile = pl.BlockSpec((TM, D_MODEL), lambda i: (i, 0))
    in_specs = [tile, _full(mod.shape), _full((1, D_MODEL)), _resident(w_in_bf.shape)]
    args = [x, mod, g_pre.reshape(1, -1), w_in_bf]
    outs = [jax.ShapeDtypeStruct((t, D_MODEL), BF16)] * 3 + [jax.ShapeDtypeStruct((t, D_MODEL), F32)] * 4
    if rope:
        n_pos = rope_tabs[0].shape[0] // TM
        tab = pl.BlockSpec((TM, LANE), lambda i: (i % n_pos, 0))
        in_specs += [tab, tab]
        args += list(rope_tabs)
    else:
        outs += [jax.ShapeDtypeStruct((t, D_MODEL), F32)] * 2
    return pl.pallas_call(
        functools.partial(_inproj_kernel, rope=rope, row0=row0, tiles_per_row=tokens_per_row // TM),
        grid=(t // TM,),
        in_specs=in_specs,
        out_specs=[tile] * len(outs),
        out_shape=outs,
        compiler_params=_cparams(("arbitrary",)),
        name="inproj_lat" if rope else "inproj_ctx",
    )(*args)


def _rope_tables(n_tokens):
    rows = n_tokens // GRID_W
    row = jnp.repeat(jnp.arange(rows, dtype=F32), GRID_W)
    col = jnp.tile(jnp.arange(GRID_W, dtype=F32), rows)
    q4 = HEAD_DIM // 4
    inv_freq = jnp.power(ROPE_THETA, -jnp.arange(q4, dtype=F32) / q4)
    ang = jnp.concatenate([row[:, None] * inv_freq, col[:, None] * inv_freq], axis=-1)
    cos, sin = jnp.cos(ang), jnp.sin(ang)
    c64 = jnp.concatenate([cos[:, :q4], cos[:, :q4], cos[:, q4:], cos[:, q4:]], axis=-1)
    s64 = jnp.concatenate([-sin[:, :q4], sin[:, :q4], -sin[:, q4:], sin[:, q4:]], axis=-1)
    return jnp.tile(c64, (1, LANE // HEAD_DIM)), jnp.tile(s64, (1, LANE // HEAD_DIM))


ONES_ROWS = 16


def _attn_kernel(lq_ref, lk_ref, gs_ref, q_ref, k_ref, v_ref, o_ref, qs_ref, m_ref, acc_ref,
                 *, tk, n_chunks, lam_init):
    qt = q_ref[0, 0]
    tq = qt.shape[1]
    sub = lax.broadcasted_iota(jnp.int32, qt.shape, 0)
    zero = jnp.zeros_like(qt)
    qs_ref[:, :tq] = jnp.where(sub < HEAD_DIM, qt, zero)
    qs_ref[:, tq:] = jnp.where(sub >= HEAD_DIM, qt, zero)
    m_ref[...] = jnp.full(m_ref.shape, -jnp.inf, F32)
    acc_ref[...] = jnp.zeros(acc_ref.shape, F32)

    def scores(c):
        kc = k_ref[0, c * tk:(c + 1) * tk, :]
        return jnp.dot(kc, qs_ref[...], preferred_element_type=F32)

    st = scores(0)
    for c in range(n_chunks):
        st_next = scores(c + 1) if c + 1 < n_chunks else None
        m_old = m_ref[...]
        m_new = jnp.maximum(m_old, jnp.max(st, axis=0, keepdims=True))
        p = jnp.exp2(st - m_new).astype(BF16)
        acc_ref[...] = jnp.exp2(m_old - m_new) * acc_ref[...] + jnp.dot(v_ref[0, 0, c], p,
                                                                        preferred_element_type=F32)
        m_ref[...] = m_new
        st = st_next

    e = jnp.exp(jnp.sum(lq_ref[...] * lk_ref[...], axis=-1, keepdims=True))
    lam = e[0:1, :] - e[1:2, :] + lam_init
    acc = acc_ref[...]
    den = acc[V_DIM:V_DIM + 1, :]
    ot = acc[:V_DIM, :tq] / den[:, :tq] - lam * (acc[:V_DIM, tq:] / den[:, tq:])
    o_ref[0] = (_rms(ot.T, gs_ref[...]) * (1.0 - lam_init)).astype(o_ref.dtype)


def _heads_t(x):
    b, n, _ = x.shape
    return x.reshape(b, n, N_HEADS, V_DIM).transpose(0, 2, 3, 1)


def _attention(q, k, v, lam_q, lam_k, g_sub, *, lam_init, tq, tk):
    b, nq, _ = q.shape
    nk = k.shape[1]
    nc = nk // tk
    qt = _heads_t(q)
    vt = jnp.concatenate([_heads_t(v), jnp.ones((b, N_HEADS, ONES_ROWS, nk), v.dtype)], axis=2)
    rows = V_DIM + ONES_ROWS
    vt = vt.reshape(b, N_HEADS, rows, nc, tk).transpose(0, 1, 3, 2, 4)
    kern = functools.partial(_attn_kernel, tk=tk, n_chunks=nc, lam_init=lam_init)
    return pl.pallas_call(
        kern,
        grid=(b, N_HEADS, nq // tq),
        in_specs=[_full((2, HEAD_DIM)), _full((2, HEAD_DIM)), _full((1, V_DIM)),
                  pl.BlockSpec((1, 1, V_DIM, tq), lambda bi, h, qi: (bi, h, 0, qi)),
                  pl.BlockSpec((1, nk, LANE), lambda bi, h, qi: (bi, 0, h)),
                  pl.BlockSpec((1, 1, nc, rows, tk), lambda bi, h, qi: (bi, h, 0, 0, 0))],
        out_specs=pl.BlockSpec((1, tq, LANE), lambda bi, h, qi: (bi, qi, h)),
        out_shape=jax.ShapeDtypeStruct((b, nq, D_MODEL), BF16),
        scratch_shapes=[pltpu.VMEM((V_DIM, 2 * tq), BF16), pltpu.VMEM((1, 2 * tq), F32),
                        pltpu.VMEM((rows, 2 * tq), F32)],
        compiler_params=_cparams(("arbitrary", "arbitrary", "arbitrary")),
        name=f"diff_attn_{nk}",
    )(lam_q, lam_k, g_sub.reshape(1, -1), qt, k, vt)


def _gelu_tanh(x):
    return 0.5 * x * (1.0 + jnp.tanh(math.sqrt(2.0 / math.pi) * (x + 0.044715 * (x * x * x))))


def _rnn_kernel(xr_ref, xg_ref, h0_ref, wc_ref, bc_ref, wg_ref, bg_ref, lam_ref,
                o_ref, hT_ref, xpad_ref, hf_ref, hb_ref, *, n, tt):
    pad = SUBLANE
    cb = xr_ref.shape[2]
    n_chunks = n // tt
    groups = tt // SUBLANE
    xpad_ref[0:pad, :] = jnp.zeros((pad, cb), F32)
    xpad_ref[pad + n:pad + n + pad, :] = jnp.zeros((pad, cb), F32)
    xpad_ref[pad:pad + n, :] = xr_ref[0]
    r8 = lax.broadcasted_iota(jnp.int32, (tt, cb), 0) & (SUBLANE - 1)

    def gate_inputs(c, d):
        t0 = pl.multiple_of(c * tt, tt)
        blk = xpad_ref[pl.ds(t0, tt + 2 * pad), :]
        ext = tt + 2 * pad
        xc = (bc_ref[...]
              + wc_ref[0:1, :] * pltpu.roll(blk, 2, 0)[pad:pad + tt]
              + wc_ref[1:2, :] * pltpu.roll(blk, 1, 0)[pad:pad + tt]
              + wc_ref[2:3, :] * blk[pad:pad + tt]
              + wc_ref[3:4, :] * pltpu.roll(blk, ext - 1, 0)[pad:pad + tt])
        g = jnp.dot(xc.astype(BF16), wg_ref[d, 0], preferred_element_type=F32)
        r = jax.nn.sigmoid(g[:, :cb] + bg_ref[d, 0:1, :])
        i = jax.nn.sigmoid(g[:, cb:] + bg_ref[d, 1:2, :])
        lam = lam_ref[d:d + 1, :]
        softplus = jnp.maximum(-lam, 0.0) + jnp.log1p(jnp.exp(-jnp.abs(lam)))
        log_a = (-LRU_C) * r * softplus
        a = jnp.exp(log_a)
        u = jnp.sqrt(-jnp.tanh(log_a) * (1.0 + a * a)) * (i * xc)
        return t0, a, u

    def fwd_chunk(c, hc):
        t0, a, u = gate_inputs(c, 0)
        for d in (1, 2, 4):
            keep = r8 >= d
            u = jnp.where(keep, a * pltpu.roll(u, d, 0) + u, u)
            a = jnp.where(keep, a * pltpu.roll(a, d, 0), a)
        for g in range(groups):
            hg = a[g * SUBLANE:(g + 1) * SUBLANE] * hc + u[g * SUBLANE:(g + 1) * SUBLANE]
            hf_ref[pl.ds(t0 + g * SUBLANE, SUBLANE), :] = hg
            hc = hg[SUBLANE - 1:SUBLANE]
        return hc

    def bwd_chunk(j, hc):
        c = n_chunks - 1 - j
        t0, a, u = gate_inputs(c, 1)
        for d in (1, 2, 4):
            keep = r8 < SUBLANE - d
            u = jnp.where(keep, a * pltpu.roll(u, tt - d, 0) + u, u)
            a = jnp.where(keep, a * pltpu.roll(a, tt - d, 0), a)
        for g in reversed(range(groups)):
            hg = a[g * SUBLANE:(g + 1) * SUBLANE] * hc + u[g * SUBLANE:(g + 1) * SUBLANE]
            hb_ref[g * SUBLANE:(g + 1) * SUBLANE, :] = hg
            hc = hg[0:1]
        hsum = hf_ref[pl.ds(t0, tt), :] + hb_ref[...]
        o_ref[0, pl.ds(t0, tt), :] = (hsum * _gelu_tanh(xg_ref[0, pl.ds(t0, tt), :])).astype(o_ref.dtype)
        return hc

    h_fwd = lax.fori_loop(0, n_chunks, fwd_chunk, h0_ref[0, 0:1, :])
    h_bwd = lax.fori_loop(0, n_chunks, bwd_chunk, h0_ref[0, 1:2, :])
    hT_ref[0, 0:1, :] = h_fwd
    hT_ref[0, 1:2, :] = h_bwd


def _rglru(xr, xg, h0, w_conv, b_conv, wg_bd, b_gate, lam):
    b, n, _ = xr.shape
    tt = min(RNN_TT, n)
    n_cb = D_RNN // RNN_CB
    slab = pl.BlockSpec((1, n, RNN_CB), lambda bi, ci: (bi, 0, ci))
    state = pl.BlockSpec((1, 2, RNN_CB), lambda bi, ci: (bi, 0, ci))
    return pl.pallas_call(
        functools.partial(_rnn_kernel, n=n, tt=tt),
        grid=(b, n_cb),
        in_specs=[slab, slab, state,
                  pl.BlockSpec((CONV_W, RNN_CB), lambda bi, ci: (0, ci)),
                  pl.BlockSpec((1, RNN_CB), lambda bi, ci: (0, ci)),
                  pl.BlockSpec((2, 1, RNN_CB, 2 * RNN_CB), lambda bi, ci: (0, ci, 0, 0)),
                  pl.BlockSpec((2, 2, RNN_CB), lambda bi, ci: (0, 0, ci)),
                  pl.BlockSpec((2, RNN_CB), lambda bi, ci: (0, ci))],
        out_specs=[slab, state],
        out_shape=[jax.ShapeDtypeStruct((b, n, D_RNN), BF16), jax.ShapeDtypeStruct((b, 2, D_RNN), F32)],
        scratch_shapes=[pltpu.VMEM((n + 2 * SUBLANE, RNN_CB), F32), pltpu.VMEM((n, RNN_CB), F32),
                        pltpu.VMEM((tt, RNN_CB), F32)],
        compiler_params=_cparams(("arbitrary", "arbitrary")),
        name=f"rglru_{n}",
    )(xr, xg, h0, w_conv, b_conv.reshape(1, -1), wg_bd, b_gate, lam)


def _gate_weights(w_gate):
    per = RNN_CB // RNN_BLOCK_DIM
    n_cb = D_RNN // RNN_CB
    w = w_gate.reshape(2, 2, n_cb, per, RNN_BLOCK_DIM, RNN_BLOCK_DIM)
    eye = jnp.eye(per, dtype=w.dtype)
    bd = jnp.einsum('dkcpij,pq->dkcpiqj', w, eye).reshape(2, 2, n_cb, RNN_CB, RNN_CB)
    return jnp.concatenate([bd[:, 0], bd[:, 1]], axis=-1).astype(BF16)


def _merge_kernel(oa_ref, or_ref, ga_ref, gr_ref, x_ref, mod_ref, base_ref, gpost_ref, gpre_ref,
                  wa_ref, wr_ref, wo_ref, wrh_ref, wrl_ref, br_ref,
                  y_ref, h_ref, info_ref, gate_ref, cnt_ref, run_ref, *, row0, tiles_per_row):
    i = pl.program_id(0)
    row = row0 + i // tiles_per_row
    ya = jnp.dot(oa_ref[...], wa_ref[...], preferred_element_type=F32)
    yr = jnp.dot(or_ref[...], wr_ref[...], preferred_element_type=F32)
    y = jax.nn.sigmoid(ga_ref[...]) * ya + jax.nn.sigmoid(gr_ref[...]) * yr
    mix = jnp.dot(y.astype(BF16), wo_ref[...], preferred_element_type=F32)
    y1 = x_ref[...] + _mod_rows(mod_ref, row, 2) * _rms(mix, gpost_ref[...])
    y_ref[...] = y1
    h = _rms(y1, gpre_ref[...]) * (1.0 + _mod_rows(mod_ref, row, 4)) + _mod_rows(mod_ref, row, 3)
    h_ref[...] = h

    h_hi = h.astype(BF16)
    h_lo = (h - h_hi.astype(F32)).astype(BF16)
    logits = (jnp.dot(h_hi, wrh_ref[...], preferred_element_type=F32)
              + jnp.dot(h_lo, wrh_ref[...], preferred_element_type=F32)
              + jnp.dot(h_hi, wrl_ref[...], preferred_element_type=F32)) + br_ref[...]
    tm = logits.shape[0]
    lane = lax.broadcasted_iota(jnp.int32, (tm, LANE), 1)
    work = logits
    sel = []
    vals = []
    for _ in range(TOP_K):
        mx = jnp.max(work, axis=-1, keepdims=True)
        idx = jnp.min(jnp.where(work == mx, lane, LANE), axis=-1, keepdims=True)
        hit = lane == idx
        sel.append((idx, hit))
        vals.append(mx)
        work = jnp.where(hit, NEG_BIG * 2.0, work)
    exps = [jnp.exp(v - vals[0]) for v in vals]
    inv = 1.0 / (exps[0] + exps[1] + exps[2] + exps[3])

    @pl.when(i == 0)
    def _():
        run_ref[...] = base_ref[...]

    mask = jnp.zeros((tm, LANE), F32)
    for _, hit in sel:
        mask = mask + hit.astype(F32)
    rr = lax.broadcasted_iota(jnp.int32, (tm, tm), 0)
    cc = lax.broadcasted_iota(jnp.int32, (tm, tm), 1)
    tri = (cc < rr).astype(BF16)
    rank = jnp.dot(tri, mask.astype(BF16), preferred_element_type=F32) + run_ref[...]
    run_ref[...] = run_ref[...] + jnp.sum(mask, axis=0, keepdims=True)
    cnt_ref[...] = run_ref[...]

    info = jnp.zeros((tm, LANE), jnp.int32)
    gates = jnp.zeros((tm, LANE), F32)
    for kk, (idx, hit) in enumerate(sel):
        rk = jnp.sum(jnp.where(hit, rank, 0.0), axis=-1, keepdims=True).astype(jnp.int32)
        info = jnp.where(lane == kk, idx, info)
        info = jnp.where(lane == TOP_K + kk, rk, info)
        gates = jnp.where(lane == kk, exps[kk] * inv, gates)
    info_ref[...] = info
    gate_ref[...] = gates


def _merge(oa, orec, ga, gr, x, mod, base, g_post, g_pre, wa, wr, wo, wr_hi, wr_lo, b_router,
           *, row0, tokens_per_row):
    t = x.shape[0]
    tile = pl.BlockSpec((TM, D_MODEL), lambda i: (i, 0))
    small = pl.BlockSpec((TM, LANE), lambda i: (i, 0))
    vec = _full((1, D_MODEL))
    wspec = _resident((D_MODEL, D_MODEL))
    rspec = _resident((D_MODEL, LANE))
    return pl.pallas_call(
        functools.partial(_merge_kernel, row0=row0, tiles_per_row=tokens_per_row // TM),
        grid=(t // TM,),
        in_specs=[tile, tile, tile, tile, tile, _full(mod.shape), _full((1, LANE)), vec, vec,
                  wspec, wspec, wspec, rspec, rspec, _full((1, LANE))],
        out_specs=[tile, tile, small, small, _full((1, LANE))],
        out_shape=[jax.ShapeDtypeStruct((t, D_MODEL), F32), jax.ShapeDtypeStruct((t, D_MODEL), F32),
                   jax.ShapeDtypeStruct((t, LANE), jnp.int32), jax.ShapeDtypeStruct((t, LANE), F32),
                   jax.ShapeDtypeStruct((1, LANE), F32)],
        scratch_shapes=[pltpu.VMEM((1, LANE), F32)],
        compiler_params=_cparams(("arbitrary",)),
        name=f"merge_router_{t}",
    )(oa, orec, ga, gr, x, mod, base, g_post.reshape(1, -1), g_pre.reshape(1, -1),
      wa, wr, wo, wr_hi, wr_lo, b_router)


TD = 128


def _dispatch_kernel(dest_ref, ps_ref, pe_ref, nb_ref, hp_ref, hs_ref, xs_ref, zero_ref, sem, zsem,
                     *, n_p_tiles, n_blocks):
    i = pl.program_id(0)

    @pl.when(i == 0)
    def _():
        zero_ref[...] = jnp.zeros(zero_ref.shape, zero_ref.dtype)

        def zero_copy(start):
            return pltpu.make_async_copy(zero_ref, xs_ref.at[pl.ds(pl.multiple_of(start, MOE_BLOCK), MOE_BLOCK), :],
                                         zsem)

        for act in ("start", "wait"):
            for e in range(N_EXPERTS):
                @pl.when(pe_ref[e] > ps_ref[e])
                def _():
                    getattr(zero_copy(pe_ref[e] - MOE_BLOCK), act)()

                @pl.when(n_blocks - 1 - e >= nb_ref[0])
                def _():
                    getattr(zero_copy((n_blocks - 1 - e) * MOE_BLOCK), act)()

    def scatter(h_ref):
        def copy(r, k):
            return pltpu.make_async_copy(h_ref.at[pl.ds(r, 1), :],
                                         xs_ref.at[pl.ds(dest_ref[r * TOP_K + k], 1), :], sem)

        def start(r, c):
            for k in range(TOP_K):
                copy(r, k).start()
            return c

        def wait(r, c):
            for k in range(TOP_K):
                copy(r, k).wait()
            return c

        lax.fori_loop(0, TD, start, 0)
        lax.fori_loop(0, TD, wait, 0)

    @pl.when(i < n_p_tiles)
    def _():
        scatter(hp_ref)

    @pl.when(i >= n_p_tiles)
    def _():
        scatter(hs_ref)


def _dispatch(dest_flat, h_p, h_s, pad_start, pad_end, n_used, n_slots):
    n_p_tiles = h_p.shape[0] // TD
    n_tiles = n_p_tiles + h_s.shape[0] // TD
    n_blocks = n_slots // MOE_BLOCK
    assert n_blocks - N_EXPERTS >= 0
    smem = pl.BlockSpec(memory_space=pltpu.SMEM)
    return pl.pallas_call(
        functools.partial(_dispatch_kernel, n_p_tiles=n_p_tiles, n_blocks=n_blocks),
        grid=(n_tiles,),
        in_specs=[pl.BlockSpec((TD * TOP_K,), lambda i: (i,), memory_space=pltpu.SMEM), smem, smem, smem,
                  pl.BlockSpec((TD, D_MODEL), lambda i: (jnp.minimum(i, n_p_tiles - 1), 0)),
                  pl.BlockSpec((TD, D_MODEL), lambda i: (jnp.maximum(i - n_p_tiles, 0), 0))],
        out_specs=pl.BlockSpec(memory_space=pl.ANY),
        out_shape=jax.ShapeDtypeStruct((n_slots, D_MODEL), h_p.dtype),
        scratch_shapes=[pltpu.VMEM((MOE_BLOCK, D_MODEL), h_p.dtype), pltpu.SemaphoreType.DMA,
                        pltpu.SemaphoreType.DMA],
        compiler_params=_cparams(("arbitrary",)),
        name="moe_dispatch",
    )(dest_flat, pad_start, pad_end, n_used, h_p, h_s)


def _expert_kernel(be_ref, nb_ref, xs_ref, wu_ref, bu_ref, wd_ref, bd_ref, ys_ref):
    i = pl.program_id(0)

    @pl.when(i < nb_ref[0])
    def _():
        hu = jnp.dot(xs_ref[...].astype(BF16), wu_ref[0], preferred_element_type=F32) + bu_ref[0]
        glu = jnp.minimum(hu[:, :D_FF], SWIGLU_LIMIT)
        lin = jnp.clip(hu[:, D_FF:], -SWIGLU_LIMIT, SWIGLU_LIMIT)
        act = (lin + 1.0) * glu * jax.nn.sigmoid(SWIGLU_ALPHA * glu)
        ys_ref[...] = jnp.dot(act.astype(BF16), wd_ref[0], preferred_element_type=F32) + bd_ref[0]

    @pl.when(i >= nb_ref[0])
    def _():
        ys_ref[...] = jnp.zeros(ys_ref.shape, F32)


def _experts(blk_e, n_used, xs, wu, bu, wd, bd):
    ns = xs.shape[0]
    grid_spec = pltpu.PrefetchScalarGridSpec(
        num_scalar_prefetch=2,
        grid=(ns // MOE_BLOCK,),
        in_specs=[pl.BlockSpec((MOE_BLOCK, D_MODEL), lambda i, be, nb: (jnp.minimum(i, nb[0] - 1), 0)),
                  pl.BlockSpec((1, D_MODEL, 2 * D_FF), lambda i, be, nb: (be[i], 0, 0)),
                  pl.BlockSpec((1, 1, 2 * D_FF), lambda i, be, nb: (be[i], 0, 0)),
                  pl.BlockSpec((1, D_FF, D_MODEL), lambda i, be, nb: (be[i], 0, 0)),
                  pl.BlockSpec((1, 1, D_MODEL), lambda i, be, nb: (be[i], 0, 0))],
        out_specs=pl.BlockSpec((MOE_BLOCK, D_MODEL), lambda i, be, nb: (i, 0)),
    )
    return pl.pallas_call(
        _expert_kernel,
        grid_spec=grid_spec,
        out_shape=jax.ShapeDtypeStruct((ns, D_MODEL), F32),
        compiler_params=_cparams(("arbitrary",)),
        name="moe_experts",
    )(blk_e, n_used, xs, wu, bu, wd, bd)


def _combine_kernel(dest_ref, ys_ref, gate_ref, y1_ref, mod_ref, gpost_ref, o_ref, buf_ref, sem,
                    *, row0, tiles_per_row):
    row = row0 + pl.program_id(0) // tiles_per_row

    def copy(r, k):
        return pltpu.make_async_copy(ys_ref.at[pl.ds(dest_ref[r * TOP_K + k], 1), :],
                                     buf_ref.at[k, pl.ds(r, 1), :], sem)

    def start(r, c):
        for k in range(TOP_K):
            copy(r, k).start()
        return c

    def wait(r, c):
        for k in range(TOP_K):
            copy(r, k).wait()
        return c

    lax.fori_loop(0, TD, start, 0)
    lax.fori_loop(0, TD, wait, 0)
    g = gate_ref[...]
    moe = g[:, 0:1] * buf_ref[0]
    for kk in range(1, TOP_K):
        moe = moe + g[:, kk:kk + 1] * buf_ref[kk]
    o_ref[...] = y1_ref[...] + _mod_rows(mod_ref, row, 5) * _rms(moe, gpost_ref[...])


def _combine(dest_flat, ys, gates, y1, mod, g_post, *, row0, tokens_per_row):
    t = y1.shape[0]
    tile = pl.BlockSpec((TD, D_MODEL), lambda i: (i, 0))
    return pl.pallas_call(
        functools.partial(_combine_kernel, row0=row0, tiles_per_row=tokens_per_row // TD),
        grid=(t // TD,),
        in_specs=[pl.BlockSpec((TD * TOP_K,), lambda i: (i,), memory_space=pltpu.SMEM),
                  pl.BlockSpec(memory_space=pl.ANY),
                  pl.BlockSpec((TD, LANE), lambda i: (i, 0)), tile, _full(mod.shape), _full((1, D_MODEL))],
        out_specs=tile,
        out_shape=jax.ShapeDtypeStruct((t, D_MODEL), F32),
        scratch_shapes=[pltpu.VMEM((TOP_K, TD, D_MODEL), F32), pltpu.SemaphoreType.DMA],
        compiler_params=_cparams(("arbitrary",)),
        name=f"moe_combine_{t}",
    )(dest_flat, ys, gates, y1, mod, g_post.reshape(1, -1))


def kernel(x_prompt, x_sample, cache_k, cache_v, state_h, c, c_ctx, w_ada, b_ada, g_pre_mix, g_post_mix, g_pre_ffn, g_post_ffn, w_in, lam_q, lam_k, g_subln, w_conv, b_conv, w_lru_gate, b_lru_gate, lru_lambda, w_attn_proj, w_rec_proj, w_out, w_router, b_router, w_up, b_up, w_down, b_down):
    depth = w_in.shape[0]
    bp, sp, _ = x_prompt.shape
    bs, ss, _ = x_sample.shape
    tp, ts = bp * sp, bs * ss
    cos_t, sin_t = _rope_tables(ss)

    y_p = x_prompt.reshape(tp, D_MODEL)
    y_s = x_sample.reshape(ts, D_MODEL)
    ks, vs, hs = [], [], []
    for l in range(depth):
        lam_init = 0.8 - 0.6 * math.exp(-0.3 * l)
        cvec = jnp.zeros((SUBLANE, D_MODEL), F32).at[0].set(c_ctx).at[1:1 + bs].set(c)
        mod = _adaln(cvec, w_ada[l], b_ada[l])
        w_in_bf = w_in[l].astype(BF16)
        wa, wr, wo = w_attn_proj[l].astype(BF16), w_rec_proj[l].astype(BF16), w_out[l].astype(BF16)
        wg_bd = _gate_weights(w_lru_gate[l])
        w_rt = jnp.pad(w_router[l], ((0, 0), (0, LANE - N_EXPERTS)))
        wr_hi = w_rt.astype(BF16)
        wr_lo = (w_rt - wr_hi.astype(F32)).astype(BF16)
        b_rt = jnp.pad(b_router[l], (0, LANE - N_EXPERTS), constant_values=NEG_BIG).reshape(1, LANE)
        wu = w_up[l].reshape(N_EXPERTS, D_MODEL, D_FF, 2)
        wu = jnp.concatenate([wu[..., 0], wu[..., 1]], axis=-1).astype(BF16)
        bu = b_up[l].reshape(N_EXPERTS, D_FF, 2)
        bu = jnp.concatenate([bu[..., 0], bu[..., 1]], axis=-1).reshape(N_EXPERTS, 1, 2 * D_FF)
        wd = w_down[l].astype(BF16)
        bd = b_down[l].reshape(N_EXPERTS, 1, D_MODEL)

        q, k, v, xr, xg, ga, gr, k32, v32 = _inproj(y_p, mod, g_pre_mix[l], w_in_bf, row0=0, tokens_per_row=tp)
        oa = _attention(q.reshape(bp, sp, D_MODEL), k.reshape(bp, sp, D_MODEL), v.reshape(bp, sp, D_MODEL),
                        lam_q[l], lam_k[l], g_subln[l], lam_init=lam_init, tq=sp, tk=sp)
        orec, h_t = _rglru(xr.reshape(bp, sp, D_RNN), xg.reshape(bp, sp, D_RNN),
                           jnp.zeros((bp, 2, D_RNN), F32), w_conv[l], b_conv[l], wg_bd, b_lru_gate[l], lru_lambda[l])
        ks.append(k32.reshape(bp, sp, N_HEADS, 2, HEAD_DIM))
        vs.append(v32.reshape(bp, sp, N_HEADS, V_DIM))
        hs.append(h_t)
        y1_p, h2_p, info_p, gate_p, cnt_p = _merge(
            oa.reshape(tp, D_MODEL), orec.reshape(tp, D_MODEL), ga, gr, y_p, mod, jnp.zeros((1, LANE), F32),
            g_post_mix[l], g_pre_ffn[l], wa, wr, wo, wr_hi, wr_lo, b_rt, row0=0, tokens_per_row=tp)

        q, k, v, xr, xg, ga, gr = _inproj(y_s, mod, g_pre_mix[l], w_in_bf, row0=1, tokens_per_row=ss,
                                          rope_tabs=(cos_t, sin_t))
        k_all = jnp.concatenate([k.reshape(bs, ss, D_MODEL),
                                 cache_k[:, l].reshape(bs, -1, D_MODEL).astype(BF16)], axis=1)
        v_all = jnp.concatenate([v.reshape(bs, ss, D_MODEL),
                                 cache_v[:, l].reshape(bs, -1, D_MODEL).astype(BF16)], axis=1)
        oa = _attention(q.reshape(bs, ss, D_MODEL), k_all, v_all, lam_q[l], lam_k[l], g_subln[l],
                        lam_init=lam_init, tq=512, tk=256)
        orec, _ = _rglru(xr.reshape(bs, ss, D_RNN), xg.reshape(bs, ss, D_RNN), state_h[:, l],
                         w_conv[l], b_conv[l], wg_bd, b_lru_gate[l], lru_lambda[l])
        y1_s, h2_s, info_s, gate_s, cnt_s = _merge(
            oa.reshape(ts, D_MODEL), orec.reshape(ts, D_MODEL), ga, gr, y_s, mod, cnt_p,
            g_post_mix[l], g_pre_ffn[l], wa, wr, wo, wr_hi, wr_lo, b_rt, row0=1, tokens_per_row=ss)

        counts = cnt_s[0, :N_EXPERTS].astype(jnp.int32)
        padded = (counts + MOE_BLOCK - 1) // MOE_BLOCK * MOE_BLOCK
        pad_end = jnp.cumsum(padded)
        pad_start = pad_end - padded
        n_slots = (tp + ts) * TOP_K + N_EXPERTS * MOE_BLOCK
        n_blocks = n_slots // MOE_BLOCK
        blk_start = jnp.arange(n_blocks, dtype=jnp.int32) * MOE_BLOCK
        blk_e = jnp.minimum(jnp.sum(pad_end[None, :] <= blk_start[:, None], axis=1), N_EXPERTS - 1).astype(jnp.int32)
        n_used = (pad_end[-1:] // MOE_BLOCK).astype(jnp.int32)

        def dest_of(info):
            e = info[:, :TOP_K]
            onehot = e[:, :, None] == jnp.arange(N_EXPERTS, dtype=jnp.int32)[None, None, :]
            return (jnp.sum(jnp.where(onehot, pad_start[None, None, :], 0), axis=-1)
                    + info[:, TOP_K:2 * TOP_K]).reshape(-1)

        dest_p, dest_s = dest_of(info_p), dest_of(info_s)
        xs = _dispatch(jnp.concatenate([dest_p, dest_s]), h2_p, h2_s, pad_start, pad_end, n_used, n_slots)
        ys = _experts(blk_e, n_used, xs, wu, bu, wd, bd)
        y_p = _combine(dest_p, ys, gate_p, y1_p, mod, g_post_ffn[l], row0=0, tokens_per_row=tp)
        y_s = _combine(dest_s, ys, gate_s, y1_s, mod, g_post_ffn[l], row0=1, tokens_per_row=ss)

    return (y_p.reshape(bp, sp, D_MODEL), y_s.reshape(bs, ss, D_MODEL),
            jnp.stack(ks, axis=1), jnp.stack(vs, axis=1), jnp.stack(hs, axis=1))
```

```python
import functools
import math

import jax
import jax.numpy as jnp
from jax import lax
from jax.experimental import pallas as pl
from jax.experimental.pallas import tpu as pltpu

F32 = jnp.float32
BF16 = jnp.bfloat16

D_MODEL = 1024
N_HEADS = 8
HEAD_DIM = 64
V_DIM = 2 * HEAD_DIM
GRID_W = 64
D_RNN = D_MODEL
RNN_BLOCKS = 16
RNN_BLOCK_DIM = D_RNN // RNN_BLOCKS
CONV_W = 4
LRU_C = 8.0
N_EXPERTS = 32
TOP_K = 4
D_FF = D_MODEL
SWIGLU_ALPHA = 1.702
SWIGLU_LIMIT = 7.0
ROPE_THETA = 10000.0
MOE_BLOCK = 256
EPS = 1e-6
N_MOD = 6
IN_PARTS = 7

LANE = 128
SUBLANE = 8
VMEM_LIMIT = 56 * 1024 * 1024

TM = 256
RNN_CB = 256
RNN_TT = 256
NEG_BIG = -1e30
ONES_ROWS = 16
V_ROWS = V_DIM + ONES_ROWS
Q_SCALE =HEAD_DIM ** -0.5 * math.log2(math.e)


def _cparams(sem):
    return pltpu.CompilerParams(dimension_semantics=sem, vmem_limit_bytes=VMEM_LIMIT)


def _full(shape):
    return pl.BlockSpec(shape, lambda *_: (0,) * len(shape))


def _resident(shape):
    return pl.BlockSpec(shape, lambda *_: (0,) * len(shape), pipeline_mode=pl.Buffered(1))


def _rms(x, g):
    return x * lax.rsqrt(jnp.mean(x * x, axis=-1, keepdims=True) + EPS) * g


def _ada_kernel(c_ref, w_ref, b_ref, o_ref):
    c = c_ref[...]
    s = c * jax.nn.sigmoid(c)
    o_ref[...] = jnp.dot(s.astype(BF16), w_ref[...].astype(BF16), preferred_element_type=F32) + b_ref[...]


def _adaln(cvec, w_ada, b_ada):
    return pl.pallas_call(
        _ada_kernel,
        grid=(N_MOD,),
        in_specs=[_full((SUBLANE, D_MODEL)),
                  pl.BlockSpec((D_MODEL, D_MODEL), lambda j: (0, j)),
                  pl.BlockSpec((1, D_MODEL), lambda j: (0, j))],
        out_specs=pl.BlockSpec((SUBLANE, D_MODEL), lambda j: (0, j)),
        out_shape=jax.ShapeDtypeStruct((SUBLANE, N_MOD * D_MODEL), F32),
        compiler_params=_cparams(("arbitrary",)),
        name="adaln",
    )(cvec, w_ada, b_ada.reshape(1, -1))


def _mod_rows(mod_ref, row, part):
    return mod_ref[pl.ds(row, 1), part * D_MODEL:(part + 1) * D_MODEL]


def _inproj_kernel(*refs, rope, row0, tiles_per_row):
    if rope:
        (x_ref, mod_ref, g_ref, w_ref, cos_ref, sin_ref,
         q_ref, k_ref, v_ref, xr_ref, xg_ref, ga_ref, gr_ref) = refs
    else:
        (x_ref, mod_ref, g_ref, w_ref,
         q_ref, k_ref, v_ref, xr_ref, xg_ref, ga_ref, gr_ref, k32_ref, v32_ref) = refs
    row = row0 + pl.program_id(0) // tiles_per_row
    shift = _mod_rows(mod_ref, row, 0)
    scale = _mod_rows(mod_ref, row, 1)
    h = (_rms(x_ref[...], g_ref[...]) * (1.0 + scale) + shift).astype(BF16)

    def proj(j):
        return jnp.dot(h, w_ref[:, j * D_MODEL:(j + 1) * D_MODEL], preferred_element_type=F32)

    def rotate(x):
        lane = lax.broadcasted_iota(jnp.int32, (x.shape[0], LANE), 1)
        first = (lane & 31) < 16
        outs = []
        for c in range(D_MODEL // LANE):
            xc = x[:, c * LANE:(c + 1) * LANE]
            partner = jnp.where(first, pltpu.roll(xc, LANE - 16, 1), pltpu.roll(xc, 16, 1))
            outs.append(xc * cos_ref[...] + partner * sin_ref[...])
        return jnp.concatenate(outs, axis=1)

    q = proj(0)
    k = proj(1)
    v = proj(2)
    if rope:
        q = rotate(q)
        k = rotate(k)
    else:
        k32_ref[...] = k
        v32_ref[...] = v
    k_ref[...] = k.astype(BF16)
    q = q * Q_SCALE
    for hd in range(N_HEADS):
        cols = slice(hd * V_DIM, (hd + 1) * V_DIM)
        q_ref[0, hd] = q[:, cols].T.astype(BF16)
        v_ref[0, hd, 0, :V_DIM, :] = v[:, cols].T.astype(BF16)
        v_ref[0, hd, 0, V_DIM:, :] = jnp.ones((ONES_ROWS, v.shape[0]), BF16)
    xr_ref[...] = proj(3)
    xg_ref[...] = proj(4)
    ga_ref[...] = proj(5)
    gr_ref[...] = proj(6)


def _inproj(x, mod, g_pre, w_in_bf, *, row0, tokens_per_row, tokens_per_batch, rope_tabs=None):
    t = x.shape[0]
    rope = rope_tabs is not None
    tile = pl.BlockSpec((TM, D_MODEL), lambda i: (i, 0))
    in_specs = [tile, _full(mod.shape), _full((1, D_MODEL)), _resident(w_in_bf.shape)]
    args = [x, mod, g_pre.reshape(1, -1), w_in_bf]
    n_b = t // tokens_per_batch
    tpb = tokens_per_batch // TM
    outs = [jax.ShapeDtypeStruct((n_b, N_HEADS, V_DIM, tokens_per_batch), BF16),
            jax.ShapeDtypeStruct((t, D_MODEL), BF16),
            jax.ShapeDtypeStruct((n_b, N_HEADS, tpb, V_ROWS, TM), BF16)] + [jax.ShapeDtypeStruct((t, D_MODEL), F32)] * 4
    out_specs = [pl.BlockSpec((1, N_HEADS, V_DIM, TM), lambda i: (i // tpb, 0, 0, i % tpb)), tile,
                 pl.BlockSpec((1, N_HEADS, 1, V_ROWS, TM), lambda i: (i // tpb, 0, i % tpb, 0, 0))] + [tile] * 4
    if rope:
        n_pos = rope_tabs[0].shape[0] // TM
        tab = pl.BlockSpec((TM, LANE), lambda i: (i % n_pos, 0))
        in_specs += [tab, tab]
        args += list(rope_tabs)
    else:
        outs += [jax.ShapeDtypeStruct((t, D_MODEL), F32)] * 2
        out_specs += [tile] * 2
    return pl.pallas_call(
        functools.partial(_inproj_kernel, rope=rope, row0=row0, tiles_per_row=tokens_per_row // TM),
        grid=(t // TM,),
        in_specs=in_specs,
        out_specs=out_specs,
        out_shape=outs,
        compiler_params=_cparams(("arbitrary",)),
        name="inproj_lat" if rope else "inproj_ctx",
    )(*args)


def _rope_tables(n_tokens):
    rows = n_tokens // GRID_W
    row = jnp.repeat(jnp.arange(rows, dtype=F32), GRID_W)
    col = jnp.tile(jnp.arange(GRID_W, dtype=F32), rows)
    q4 = HEAD_DIM // 4
    inv_freq = jnp.power(ROPE_THETA, -jnp.arange(q4, dtype=F32) / q4)
    ang = jnp.concatenate([row[:, None] * inv_freq, col[:, None] * inv_freq], axis=-1)
    cos, sin = jnp.cos(ang), jnp.sin(ang)
    c64 = jnp.concatenate([cos[:, :q4], cos[:, :q4], cos[:, q4:], cos[:, q4:]], axis=-1)
    s64 = jnp.concatenate([-sin[:, :q4], sin[:, :q4], -sin[:, q4:], sin[:, q4:]], axis=-1)
    return jnp.tile(c64, (1, LANE // HEAD_DIM)), jnp.tile(s64, (1, LANE // HEAD_DIM))


def _attn_kernel(*refs, tk, n_new, n_cache, lam_init):
    if n_cache:
        lq_ref, lk_ref, gs_ref, q_ref, k_ref, v_ref, kc_ref, vc_ref, o_ref, qs_ref, m_ref, acc_ref = refs
    else:
        lq_ref, lk_ref, gs_ref, q_ref, k_ref, v_ref, o_ref, qs_ref, m_ref, acc_ref = refs
    n_chunks = n_new + n_cache
    qt = q_ref[0, 0]
    tq = qt.shape[1]
    sub = lax.broadcasted_iota(jnp.int32, qt.shape, 0)
    zero = jnp.zeros_like(qt)
    qs_ref[:, :tq] = jnp.where(sub < HEAD_DIM, qt, zero)
    qs_ref[:, tq:] = jnp.where(sub >= HEAD_DIM, qt, zero)
    m_ref[...] = jnp.full(m_ref.shape, -jnp.inf, F32)
    acc_ref[...] = jnp.zeros(acc_ref.shape, F32)

    def scores(c):
        if c < n_new:
            kc = k_ref[0, c * tk:(c + 1) * tk, :]
        else:
            kc = kc_ref[0, (c - n_new) * tk:(c - n_new + 1) * tk, :]
        return jnp.dot(kc, qs_ref[...], preferred_element_type=F32)

    def values(c):
        return v_ref[0, 0, c] if c < n_new else vc_ref[0, 0, c - n_new]

    st = scores(0)
    for c in range(n_chunks):
        st_next = scores(c + 1) if c + 1 < n_chunks else None
        m_old = m_ref[...]
        m_new = jnp.maximum(m_old, jnp.max(st, axis=0, keepdims=True))
        p = jnp.exp2(st - m_new).astype(BF16)
        acc_ref[...] = jnp.exp2(m_old - m_new) * acc_ref[...] + jnp.dot(values(c), p, preferred_element_type=F32)
        m_ref[...] = m_new
        st = st_next

    e = jnp.exp(jnp.sum(lq_ref[...] * lk_ref[...], axis=-1, keepdims=True))
    lam = e[0:1, :] - e[1:2, :] + lam_init
    acc = acc_ref[...]
    den = acc[V_DIM:V_DIM + 1, :]
    ot = acc[:V_DIM, :tq] / den[:, :tq] - lam * (acc[:V_DIM, tq:] / den[:, tq:])
    o_ref[0] = (_rms(ot.T, gs_ref[...]) * (1.0 - lam_init)).astype(o_ref.dtype)


def _cache_layouts(cache_k, cache_v, tk):
    b, p = cache_k.shape[:2]
    kc = cache_k.reshape(b, p, D_MODEL).astype(BF16)
    vt = cache_v.transpose(0, 2, 3, 1)
    vt = jnp.concatenate([vt, jnp.ones((b, N_HEADS, ONES_ROWS, p), vt.dtype)], axis=2)
    return kc, vt.reshape(b, N_HEADS, V_ROWS, p // tk, tk).transpose(0, 1, 3, 2, 4).astype(BF16)


def _attention(qt, k, vt, lam_q, lam_k, g_sub, *, lam_init, tq, cache=None):
    b, _, _, nq = qt.shape
    n = k.shape[1]
    tk = vt.shape[-1]
    n_new = n // tk
    in_specs = [_full((2, HEAD_DIM)), _full((2, HEAD_DIM)), _full((1, V_DIM)),
                pl.BlockSpec((1, 1, V_DIM, tq), lambda bi, h, qi: (bi, h, 0, qi)),
                pl.BlockSpec((1, n, LANE), lambda bi, h, qi: (bi, 0, h)),
                pl.BlockSpec((1, 1, n_new, V_ROWS, tk), lambda bi, h, qi: (bi, h, 0, 0, 0))]
    args = [lam_q, lam_k, g_sub.reshape(1, -1), qt, k, vt]
    n_cache = 0
    if cache is not None:
        kc, vc = cache
        n_cache = vc.shape[2]
        in_specs += [pl.BlockSpec((1, kc.shape[1], LANE), lambda bi, h, qi: (bi, 0, h)),
                     pl.BlockSpec((1, 1, n_cache, V_ROWS, tk), lambda bi, h, qi: (bi, h, 0, 0, 0))]
        args += [kc, vc]
    kern = functools.partial(_attn_kernel, tk=tk, n_new=n_new, n_cache=n_cache, lam_init=lam_init)
    return pl.pallas_call(
        kern,
        grid=(b, N_HEADS, nq // tq),
        in_specs=in_specs,
        out_specs=pl.BlockSpec((1, tq, LANE), lambda bi, h, qi: (bi, qi, h)),
        out_shape=jax.ShapeDtypeStruct((b, nq, D_MODEL), BF16),
        scratch_shapes=[pltpu.VMEM((V_DIM, 2 * tq), BF16), pltpu.VMEM((1, 2 * tq), F32),
                        pltpu.VMEM((V_ROWS, 2 * tq), F32)],
        compiler_params=_cparams(("arbitrary", "arbitrary", "arbitrary")),
        name=f"diff_attn_{n + n_cache * tk}",
    )(*args)


def _sigmoid(x):
    return 0.5 * jnp.tanh(0.5 * x) + 0.5


def _gelu_tanh(x):
    return 0.5 * x * (1.0 + jnp.tanh(math.sqrt(2.0 / math.pi) * (x + 0.044715 * (x * x * x))))


def _rnn_kernel(xr_ref, xg_ref, h0_ref, wc_ref, bc_ref, wg_ref, bg_ref, lam_ref,
                o_ref, hT_ref, xpad_ref, hf_ref, hb_ref, *, n, tt):
    pad = SUBLANE
    cb = xr_ref.shape[2]
    n_chunks = n // tt
    groups = tt // SUBLANE
    xpad_ref[0:pad, :] = jnp.zeros((pad, cb), F32)
    xpad_ref[pad + n:pad + n + pad, :] = jnp.zeros((pad, cb), F32)
    xpad_ref[pad:pad + n, :] = xr_ref[0]
    r8 = lax.broadcasted_iota(jnp.int32, (groups, SUBLANE, cb), 1)

    def gate_inputs(c, d):
        t0 = pl.multiple_of(c * tt, tt)
        blk = xpad_ref[pl.ds(t0, tt + 2 * pad), :]
        ext = tt + 2 * pad
        xc = (bc_ref[...]
              + wc_ref[0:1, :] * pltpu.roll(blk, 2, 0)[pad:pad + tt]
              + wc_ref[1:2, :] * pltpu.roll(blk, 1, 0)[pad:pad + tt]
              + wc_ref[2:3, :] * blk[pad:pad + tt]
              + wc_ref[3:4, :] * pltpu.roll(blk, ext - 1, 0)[pad:pad + tt])
        g = jnp.dot(xc.astype(BF16), wg_ref[d, 0], preferred_element_type=F32)
        r = _sigmoid(g[:, :cb] + bg_ref[d, 0:1, :])
        i = _sigmoid(g[:, cb:] + bg_ref[d, 1:2, :])
        lam = lam_ref[d:d + 1, :]
        softplus = jnp.maximum(-lam, 0.0) + jnp.log1p(jnp.exp(-jnp.abs(lam)))
        log_a = (-LRU_C) * r * softplus
        a = jnp.exp(log_a)
        u = jnp.sqrt(-jnp.tanh(log_a) * (1.0 + a * a)) * (i * xc)
        return t0, a, u

    def scan8(a, u, reverse):
        a = a.reshape(groups, SUBLANE, cb)
        u = u.reshape(groups, SUBLANE, cb)
        for d in (1, 2, 4):
            keep = (r8 < SUBLANE - d) if reverse else (r8 >= d)
            shift = SUBLANE - d if reverse else d
            u = u + jnp.where(keep, a * pltpu.roll(u, shift, 1), 0.0)
            a = jnp.where(keep, a * pltpu.roll(a, shift, 1), a)
        return a, u

    def fwd_chunk(c, hc):
        t0, a, u = gate_inputs(c, 0)
        a, u = scan8(a, u, False)
        for g in range(groups):
            hg = a[g] * hc + u[g]
            hf_ref[pl.ds(t0 + g * SUBLANE, SUBLANE), :] = hg
            hc = hg[SUBLANE - 1:SUBLANE]
        return hc

    def bwd_chunk(j, hc):
        c = n_chunks - 1 - j
        t0, a, u = gate_inputs(c, 1)
        a, u = scan8(a, u, True)
        for g in reversed(range(groups)):
            hg = a[g] * hc + u[g]
            hb_ref[g * SUBLANE:(g + 1) * SUBLANE, :] = hg
            hc = hg[0:1]
        hsum = hf_ref[pl.ds(t0, tt), :] + hb_ref[...]
        o_ref[0, pl.ds(t0, tt), :] = (hsum * _gelu_tanh(xg_ref[0, pl.ds(t0, tt), :])).astype(o_ref.dtype)
        return hc

    h_fwd = lax.fori_loop(0, n_chunks, fwd_chunk, h0_ref[0, 0:1, :])
    h_bwd = lax.fori_loop(0, n_chunks, bwd_chunk, h0_ref[0, 1:2, :])
    hT_ref[0, 0:1, :] = h_fwd
    hT_ref[0, 1:2, :] = h_bwd


def _rglru(xr, xg, h0, w_conv, b_conv, wg_bd, b_gate, lam):
    b, n, _ = xr.shape
    tt = min(RNN_TT, n)
    n_cb = D_RNN // RNN_CB
    slab = pl.BlockSpec((1, n, RNN_CB), lambda bi, ci: (bi, 0, ci))
    state = pl.BlockSpec((1, 2, RNN_CB), lambda bi, ci: (bi, 0, ci))
    return pl.pallas_call(
        functools.partial(_rnn_kernel, n=n, tt=tt),
        grid=(b, n_cb),
        in_specs=[slab, slab, state,
                  pl.BlockSpec((CONV_W, RNN_CB), lambda bi, ci: (0, ci)),
                  pl.BlockSpec((1, RNN_CB), lambda bi, ci: (0, ci)),
                  pl.BlockSpec((2, 1, RNN_CB, 2 * RNN_CB), lambda bi, ci: (0, ci, 0, 0)),
                  pl.BlockSpec((2, 2, RNN_CB), lambda bi, ci: (0, 0, ci)),
                  pl.BlockSpec((2, RNN_CB), lambda bi, ci: (0, ci))],
        out_specs=[slab, state],
        out_shape=[jax.ShapeDtypeStruct((b, n, D_RNN), BF16), jax.ShapeDtypeStruct((b, 2, D_RNN), F32)],
        scratch_shapes=[pltpu.VMEM((n + 2 * SUBLANE, RNN_CB), F32), pltpu.VMEM((n, RNN_CB), F32),
                        pltpu.VMEM((tt, RNN_CB), F32)],
        compiler_params=_cparams(("arbitrary", "arbitrary")),
        name=f"rglru_{n}",
    )(xr, xg, h0, w_conv, b_conv.reshape(1, -1), wg_bd, b_gate, lam)


def _gate_weights(w_gate):
    per = RNN_CB // RNN_BLOCK_DIM
    n_cb = D_RNN // RNN_CB
    w = w_gate.reshape(2, 2, n_cb, per, RNN_BLOCK_DIM, RNN_BLOCK_DIM)
    eye = jnp.eye(per, dtype=w.dtype)
    bd = jnp.einsum('dkcpij,pq->dkcpiqj', w, eye).reshape(2, 2, n_cb, RNN_CB, RNN_CB)
    return jnp.concatenate([bd[:, 0], bd[:, 1]], axis=-1).astype(BF16)


def _merge_kernel(oa_ref, or_ref, ga_ref, gr_ref, x_ref, mod_ref, base_ref, gpost_ref, gpre_ref,
                  wa_ref, wr_ref, wo_ref, wrh_ref, wrl_ref, br_ref,
                  y_ref, h_ref, info_ref, gate_ref, cnt_ref, run_ref, *, row0, tiles_per_row):
    i = pl.program_id(0)
    row = row0 + i // tiles_per_row
    ya = jnp.dot(oa_ref[...], wa_ref[...], preferred_element_type=F32)
    yr = jnp.dot(or_ref[...], wr_ref[...], preferred_element_type=F32)
    y = _sigmoid(ga_ref[...]) * ya + _sigmoid(gr_ref[...]) * yr
    mix = jnp.dot(y.astype(BF16), wo_ref[...], preferred_element_type=F32)
    y1 = x_ref[...] + _mod_rows(mod_ref, row, 2) * _rms(mix, gpost_ref[...])
    y_ref[...] = y1
    h = _rms(y1, gpre_ref[...]) * (1.0 + _mod_rows(mod_ref, row, 4)) + _mod_rows(mod_ref, row, 3)
    h_ref[...] = h

    h_hi = h.astype(BF16)
    h_lo = (h - h_hi.astype(F32)).astype(BF16)
    logits = (jnp.dot(h_hi, wrh_ref[...], preferred_element_type=F32)
              + jnp.dot(h_lo, wrh_ref[...], preferred_element_type=F32)
              + jnp.dot(h_hi, wrl_ref[...], preferred_element_type=F32)) + br_ref[...]
    tm = logits.shape[0]
    lane = lax.broadcasted_iota(jnp.int32, (tm, LANE), 1)
    work = logits
    sel = []
    vals = []
    for _ in range(TOP_K):
        mx = jnp.max(work, axis=-1, keepdims=True)
        idx = jnp.min(jnp.where(work == mx, lane, LANE), axis=-1, keepdims=True)
        hit = lane == idx
        sel.append((idx, hit))
        vals.append(mx)
        work = jnp.where(hit, NEG_BIG * 2.0, work)
    exps = [jnp.exp(v - vals[0]) for v in vals]
    inv = 1.0 / (exps[0] + exps[1] + exps[2] + exps[3])

    @pl.when(i == 0)
    def _():
        run_ref[...] = base_ref[...]

    mask = jnp.zeros((tm, LANE), F32)
    for _, hit in sel:
        mask = mask + hit.astype(F32)
    rr = lax.broadcasted_iota(jnp.int32, (tm, tm), 0)
    cc = lax.broadcasted_iota(jnp.int32, (tm, tm), 1)
    tri = (cc < rr).astype(BF16)
    rank = jnp.dot(tri, mask.astype(BF16), preferred_element_type=F32) + run_ref[...]
    run_ref[...] = run_ref[...] + jnp.sum(mask, axis=0, keepdims=True)
    cnt_ref[...] = run_ref[...]

    info = jnp.zeros((tm, LANE), jnp.int32)
    gates = jnp.zeros((tm, LANE), F32)
    for kk, (idx, hit) in enumerate(sel):
        rk = jnp.sum(jnp.where(hit, rank, 0.0), axis=-1, keepdims=True).astype(jnp.int32)
        info = jnp.where(lane == kk, idx, info)
        info = jnp.where(lane == TOP_K + kk, rk, info)
        gates = jnp.where(lane == kk, exps[kk] * inv, gates)
    info_ref[...] = info
    gate_ref[...] = gates


def _merge(oa, orec, ga, gr, x, mod, base, g_post, g_pre, wa, wr, wo, wr_hi, wr_lo, b_router,
           *, row0, tokens_per_row):
    t = x.shape[0]
    tile = pl.BlockSpec((TM, D_MODEL), lambda i: (i, 0))
    small = pl.BlockSpec((TM, LANE), lambda i: (i, 0))
    vec = _full((1, D_MODEL))
    wspec = _resident((D_MODEL, D_MODEL))
    rspec = _resident((D_MODEL, LANE))
    return pl.pallas_call(
        functools.partial(_merge_kernel, row0=row0, tiles_per_row=tokens_per_row // TM),
        grid=(t // TM,),
        in_specs=[tile, tile, tile, tile, tile, _full(mod.shape), _full((1, LANE)), vec, vec,
                  wspec, wspec, wspec, rspec, rspec, _full((1, LANE))],
        out_specs=[tile, tile, small, small, _full((1, LANE))],
        out_shape=[jax.ShapeDtypeStruct((t, D_MODEL), F32), jax.ShapeDtypeStruct((t, D_MODEL), F32),
                   jax.ShapeDtypeStruct((t, LANE), jnp.int32), jax.ShapeDtypeStruct((t, LANE), F32),
                   jax.ShapeDtypeStruct((1, LANE), F32)],
        scratch_shapes=[pltpu.VMEM((1, LANE), F32)],
        compiler_params=_cparams(("arbitrary",)),
        name=f"merge_router_{t}",
    )(oa, orec, ga, gr, x, mod, base, g_post.reshape(1, -1), g_pre.reshape(1, -1),
      wa, wr, wo, wr_hi, wr_lo, b_router)


TD = 128


def _dispatch_kernel(dest_ref, ps_ref, pe_ref, nb_ref, hp_ref, hs_ref, xs_ref, zero_ref, sem, zsem,
                     *, n_p_tiles, n_blocks):
    i = pl.program_id(0)

    @pl.when(i == 0)
    def _():
        zero_ref[...] = jnp.zeros(zero_ref.shape, zero_ref.dtype)

        def zero_copy(start):
            return pltpu.make_async_copy(zero_ref, xs_ref.at[pl.ds(pl.multiple_of(start, MOE_BLOCK), MOE_BLOCK), :],
                                         zsem)

        for act in ("start", "wait"):
            for e in range(N_EXPERTS):
                @pl.when(pe_ref[e] > ps_ref[e])
                def _():
                    getattr(zero_copy(pe_ref[e] - MOE_BLOCK), act)()

                @pl.when(n_blocks - 1 - e >= nb_ref[0])
                def _():
                    getattr(zero_copy((n_blocks - 1 - e) * MOE_BLOCK), act)()

    def scatter(h_ref):
        def copy(r, k):
            return pltpu.make_async_copy(h_ref.at[pl.ds(r, 1), :],
                                         xs_ref.at[pl.ds(dest_ref[r * TOP_K + k], 1), :], sem)

        def start(r, c):
            for k in range(TOP_K):
                copy(r, k).start()
            return c

        def wait(r, c):
            for k in range(TOP_K):
                copy(r, k).wait()
            return c

        lax.fori_loop(0, TD, start, 0)
        lax.fori_loop(0, TD, wait, 0)

    @pl.when(i < n_p_tiles)
    def _():
        scatter(hp_ref)

    @pl.when(i >= n_p_tiles)
    def _():
        scatter(hs_ref)


def _dispatch(dest_flat, h_p, h_s, pad_start, pad_end, n_used, n_slots):
    n_p_tiles = h_p.shape[0] // TD
    n_tiles = n_p_tiles + h_s.shape[0] // TD
    n_blocks = n_slots // MOE_BLOCK
    assert n_blocks - N_EXPERTS >= 0
    smem = pl.BlockSpec(memory_space=pltpu.SMEM)
    return pl.pallas_call(
        functools.partial(_dispatch_kernel, n_p_tiles=n_p_tiles, n_blocks=n_blocks),
        grid=(n_tiles,),
        in_specs=[pl.BlockSpec((TD * TOP_K,), lambda i: (i,), memory_space=pltpu.SMEM), smem, smem, smem,
                  pl.BlockSpec((TD, D_MODEL), lambda i: (jnp.minimum(i, n_p_tiles - 1), 0)),
                  pl.BlockSpec((TD, D_MODEL), lambda i: (jnp.maximum(i - n_p_tiles, 0), 0))],
        out_specs=pl.BlockSpec(memory_space=pl.ANY),
        out_shape=jax.ShapeDtypeStruct((n_slots, D_MODEL), h_p.dtype),
        scratch_shapes=[pltpu.VMEM((MOE_BLOCK, D_MODEL), h_p.dtype), pltpu.SemaphoreType.DMA,
                        pltpu.SemaphoreType.DMA],
        compiler_params=_cparams(("arbitrary",)),
        name="moe_dispatch",
    )(dest_flat, pad_start, pad_end, n_used, h_p, h_s)


SEL_W = 256


def _expert_kernel(be_ref, nb_ref, xs_ref, wu_ref, bu_ref, wd_ref, bd_ref, ys_ref, wub_ref, wdb_ref):
    i = pl.program_id(0)
    used = i < nb_ref[0]
    new_expert = (i == 0) | (be_ref[i] != be_ref[jnp.maximum(i - 1, 0)])

    @pl.when(used & new_expert)
    def _():
        r = lax.broadcasted_iota(jnp.int32, (SEL_W, SEL_W), 0)
        c = lax.broadcasted_iota(jnp.int32, (SEL_W, SEL_W), 1)
        half = SEL_W // 2
        sel = jnp.where(r == jnp.where(c < half, 2 * c, 2 * (c - half) + 1), 1.0, 0.0).astype(BF16)
        for g in range(2 * D_FF // SEL_W):
            cols = wu_ref[0, :, g * SEL_W:(g + 1) * SEL_W].astype(BF16)
            d = jnp.dot(cols, sel, preferred_element_type=F32).astype(BF16)
            wub_ref[:, g * half:(g + 1) * half] = d[:, :half]
            wub_ref[:, D_FF + g * half:D_FF + (g + 1) * half] = d[:, half:]
        wdb_ref[...] = wd_ref[0].astype(BF16)

    @pl.when(used)
    def _():
        hu = jnp.dot(xs_ref[...].astype(BF16), wub_ref[...], preferred_element_type=F32) + bu_ref[0]
        glu = jnp.minimum(hu[:, :D_FF], SWIGLU_LIMIT)
        lin = jnp.clip(hu[:, D_FF:], -SWIGLU_LIMIT, SWIGLU_LIMIT)
        act = (lin + 1.0) * glu * _sigmoid(SWIGLU_ALPHA * glu)
        ys_ref[...] = jnp.dot(act.astype(BF16), wdb_ref[...], preferred_element_type=F32) + bd_ref[0]

    @pl.when(i >= nb_ref[0])
    def _():
        ys_ref[...] = jnp.zeros(ys_ref.shape, F32)


def _experts(blk_e, n_used, xs, wu, bu, wd, bd):
    ns = xs.shape[0]
    grid_spec = pltpu.PrefetchScalarGridSpec(
        num_scalar_prefetch=2,
        grid=(ns // MOE_BLOCK,),
        in_specs=[pl.BlockSpec((MOE_BLOCK, D_MODEL), lambda i, be, nb: (jnp.minimum(i, nb[0] - 1), 0)),
                  pl.BlockSpec((1, D_MODEL, 2 * D_FF), lambda i, be, nb: (be[i], 0, 0)),
                  pl.BlockSpec((1, 1, 2 * D_FF), lambda i, be, nb: (be[i], 0, 0)),
                  pl.BlockSpec((1, D_FF, D_MODEL), lambda i, be, nb: (be[i], 0, 0)),
                  pl.BlockSpec((1, 1, D_MODEL), lambda i, be, nb: (be[i], 0, 0))],
        out_specs=pl.BlockSpec((MOE_BLOCK, D_MODEL), lambda i, be, nb: (i, 0)),
        scratch_shapes=[pltpu.VMEM((D_MODEL, 2 * D_FF), BF16), pltpu.VMEM((D_FF, D_MODEL), BF16)],
    )
    return pl.pallas_call(
        _expert_kernel,
        grid_spec=grid_spec,
        out_shape=jax.ShapeDtypeStruct((ns, D_MODEL), F32),
        compiler_params=_cparams(("arbitrary",)),
        name="moe_experts",
    )(blk_e, n_used, xs, wu, bu, wd, bd)


def _combine_kernel(dest_ref, ys_ref, gate_ref, y1_ref, mod_ref, gpost_ref, o_ref, buf_ref, sem,
                    *, row0, tiles_per_row):
    row = row0 + pl.program_id(0) // tiles_per_row

    def copy(r, k):
        return pltpu.make_async_copy(ys_ref.at[pl.ds(dest_ref[r * TOP_K + k], 1), :],
                                     buf_ref.at[k, pl.ds(r, 1), :], sem)

    def start(r, c):
        for k in range(TOP_K):
            copy(r, k).start()
        return c

    def wait(r, c):
        for k in range(TOP_K):
            copy(r, k).wait()
        return c

    lax.fori_loop(0, TD, start, 0)
    lax.fori_loop(0, TD, wait, 0)
    g = gate_ref[...]
    moe = g[:, 0:1] * buf_ref[0]
    for kk in range(1, TOP_K):
        moe = moe + g[:, kk:kk + 1] * buf_ref[kk]
    o_ref[...] = y1_ref[...] + _mod_rows(mod_ref, row, 5) * _rms(moe, gpost_ref[...])


def _combine(dest_flat, ys, gates, y1, mod, g_post, *, row0, tokens_per_row):
    t = y1.shape[0]
    tile = pl.BlockSpec((TD, D_MODEL), lambda i: (i, 0))
    return pl.pallas_call(
        functools.partial(_combine_kernel, row0=row0, tiles_per_row=tokens_per_row // TD),
        grid=(t // TD,),
        in_specs=[pl.BlockSpec((TD * TOP_K,), lambda i: (i,), memory_space=pltpu.SMEM),
                  pl.BlockSpec(memory_space=pl.ANY),
                  pl.BlockSpec((TD, LANE), lambda i: (i, 0)), tile, _full(mod.shape), _full((1, D_MODEL))],
        out_specs=tile,
        out_shape=jax.ShapeDtypeStruct((t, D_MODEL), F32),
        scratch_shapes=[pltpu.VMEM((TOP_K, TD, D_MODEL), F32), pltpu.SemaphoreType.DMA],
        compiler_params=_cparams(("arbitrary",)),
        name=f"moe_combine_{t}",
    )(dest_flat, ys, gates, y1, mod, g_post.reshape(1, -1))


def kernel(x_prompt, x_sample, cache_k, cache_v, state_h, c, c_ctx, w_ada, b_ada, g_pre_mix, g_post_mix, g_pre_ffn, g_post_ffn, w_in, lam_q, lam_k, g_subln, w_conv, b_conv, w_lru_gate, b_lru_gate, lru_lambda, w_attn_proj, w_rec_proj, w_out, w_router, b_router, w_up, b_up, w_down, b_down):
    depth = w_in.shape[0]
    bp, sp, _ = x_prompt.shape
    bs, ss, _ = x_sample.shape
    tp, ts = bp * sp, bs * ss
    cos_t, sin_t = _rope_tables(ss)

    y_p = x_prompt.reshape(tp, D_MODEL)
    y_s = x_sample.reshape(ts, D_MODEL)
    ks, vs, hs = [], [], []
    for l in range(depth):
        lam_init = 0.8 - 0.6 * math.exp(-0.3 * l)
        cvec = jnp.zeros((SUBLANE, D_MODEL), F32).at[0].set(c_ctx).at[1:1 + bs].set(c)
        mod = _adaln(cvec, w_ada[l], b_ada[l])
        w_in_bf = w_in[l].astype(BF16)
        wa, wr, wo = w_attn_proj[l].astype(BF16), w_rec_proj[l].astype(BF16), w_out[l].astype(BF16)
        wg_bd = _gate_weights(w_lru_gate[l])
        w_rt = jnp.pad(w_router[l], ((0, 0), (0, LANE - N_EXPERTS)))
        wr_hi = w_rt.astype(BF16)
        wr_lo = (w_rt - wr_hi.astype(F32)).astype(BF16)
        b_rt = jnp.pad(b_router[l], (0, LANE - N_EXPERTS), constant_values=NEG_BIG).reshape(1, LANE)
        bu = b_up[l].reshape(N_EXPERTS, D_FF, 2)
        bu = jnp.concatenate([bu[..., 0], bu[..., 1]], axis=-1).reshape(N_EXPERTS, 1, 2 * D_FF)
        bd = b_down[l].reshape(N_EXPERTS, 1, D_MODEL)

        qt, k, vt, xr, xg, ga, gr, k32, v32 = _inproj(y_p, mod, g_pre_mix[l], w_in_bf, row0=0, tokens_per_row=tp,
                                                      tokens_per_batch=sp)
        oa = _attention(qt, k.reshape(bp, sp, D_MODEL), vt, lam_q[l], lam_k[l], g_subln[l],
                        lam_init=lam_init, tq=sp)
        orec, h_t = _rglru(xr.reshape(bp, sp, D_RNN), xg.reshape(bp, sp, D_RNN),
                           jnp.zeros((bp, 2, D_RNN), F32), w_conv[l], b_conv[l], wg_bd, b_lru_gate[l], lru_lambda[l])
        ks.append(k32.reshape(bp, sp, N_HEADS, 2, HEAD_DIM))
        vs.append(v32.reshape(bp, sp, N_HEADS, V_DIM))
        hs.append(h_t)
        y1_p, h2_p, info_p, gate_p, cnt_p = _merge(
            oa.reshape(tp, D_MODEL), orec.reshape(tp, D_MODEL), ga, gr, y_p, mod, jnp.zeros((1, LANE), F32),
            g_post_mix[l], g_pre_ffn[l], wa, wr, wo, wr_hi, wr_lo, b_rt, row0=0, tokens_per_row=tp)

        qt, k, vt, xr, xg, ga, gr = _inproj(y_s, mod, g_pre_mix[l], w_in_bf, row0=1, tokens_per_row=ss,
                                            tokens_per_batch=ss, rope_tabs=(cos_t, sin_t))
        oa = _attention(qt, k.reshape(bs, ss, D_MODEL), vt, lam_q[l], lam_k[l], g_subln[l],
                        lam_init=lam_init, tq=512, cache=_cache_layouts(cache_k[:, l], cache_v[:, l], TM))
        orec, _ = _rglru(xr.reshape(bs, ss, D_RNN), xg.reshape(bs, ss, D_RNN), state_h[:, l],
                         w_conv[l], b_conv[l], wg_bd, b_lru_gate[l], lru_lambda[l])
        y1_s, h2_s, info_s, gate_s, cnt_s = _merge(
            oa.reshape(ts, D_MODEL), orec.reshape(ts, D_MODEL), ga, gr, y_s, mod, cnt_p,
            g_post_mix[l], g_pre_ffn[l], wa, wr, wo, wr_hi, wr_lo, b_rt, row0=1, tokens_per_row=ss)

        counts = cnt_s[0, :N_EXPERTS].astype(jnp.int32)
        padded = (counts + MOE_BLOCK - 1) // MOE_BLOCK * MOE_BLOCK
        pad_end = jnp.cumsum(padded)
        pad_start = pad_end - padded
        n_slots = (tp + ts) * TOP_K + N_EXPERTS * MOE_BLOCK
        n_blocks = n_slots // MOE_BLOCK
        blk_start = jnp.arange(n_blocks, dtype=jnp.int32) * MOE_BLOCK
        blk_e = jnp.minimum(jnp.sum(pad_end[None, :] <= blk_start[:, None], axis=1), N_EXPERTS - 1).astype(jnp.int32)
        n_used = (pad_end[-1:] // MOE_BLOCK).astype(jnp.int32)

        def dest_of(info):
            e = info[:, :TOP_K]
            onehot = e[:, :, None] == jnp.arange(N_EXPERTS, dtype=jnp.int32)[None, None, :]
            return (jnp.sum(jnp.where(onehot, pad_start[None, None, :], 0), axis=-1)
                    + info[:, TOP_K:2 * TOP_K]).reshape(-1)

        dest_p, dest_s = dest_of(info_p), dest_of(info_s)
        xs = _dispatch(jnp.concatenate([dest_p, dest_s]), h2_p, h2_s, pad_start, pad_end, n_used, n_slots)
        ys = _experts(blk_e, n_used, xs, w_up[l], bu, w_down[l], bd)
        y_p = _combine(dest_p, ys, gate_p, y1_p, mod, g_post_ffn[l], row0=0, tokens_per_row=tp)
        y_s = _combine(dest_s, ys, gate_s, y1_s, mod, g_post_ffn[l], row0=1, tokens_per_row=ss)

    return (y_p.reshape(bp, sp, D_MODEL), y_s.reshape(bs, ss, D_MODEL),
            jnp.stack(ks, axis=1), jnp.stack(vs, axis=1), jnp.stack(hs, axis=1))
```

```python
import functools
import math

import jax
import jax.numpy as jnp
from jax import lax
from jax.experimental import pallas as pl
from jax.experimental.pallas import tpu as pltpu

F32 = jnp.float32
BF16 = jnp.bfloat16

D_MODEL = 1024
N_HEADS = 8
HEAD_DIM = 64
V_DIM = 2 * HEAD_DIM
GRID_W = 64
D_RNN = D_MODEL
RNN_BLOCKS = 16
RNN_BLOCK_DIM = D_RNN // RNN_BLOCKS
CONV_W = 4
LRU_C = 8.0
N_EXPERTS = 32
TOP_K = 4
D_FF = D_MODEL
SWIGLU_ALPHA = 1.702
SWIGLU_LIMIT = 7.0
ROPE_THETA = 10000.0
MOE_BLOCK = 256
EPS = 1e-6
N_MOD = 6
IN_PARTS = 7

LANE = 128
SUBLANE = 8
VMEM_LIMIT = 56 * 1024 * 1024

TM = 256
RNN_CB = 256
RNN_TT = 256
NEG_BIG = -1e30
ONES_ROWS = 16
V_ROWS = V_DIM + ONES_ROWS
Q_SCALE =HEAD_DIM ** -0.5 * math.log2(math.e)


def _cparams(sem):
    return pltpu.CompilerParams(dimension_semantics=sem, vmem_limit_bytes=VMEM_LIMIT)


def _full(shape):
    return pl.BlockSpec(shape, lambda *_: (0,) * len(shape))


def _resident(shape):
    return pl.BlockSpec(shape, lambda *_: (0,) * len(shape), pipeline_mode=pl.Buffered(1))


def _rms(x, g):
    return x * lax.rsqrt(jnp.mean(x * x, axis=-1, keepdims=True) + EPS) * g


def _ada_kernel(c_ref, w_ref, b_ref, o_ref):
    c = c_ref[...]
    s = c * jax.nn.sigmoid(c)
    o_ref[...] = jnp.dot(s.astype(BF16), w_ref[...].astype(BF16), preferred_element_type=F32) + b_ref[...]


def _adaln(cvec, w_ada, b_ada):
    return pl.pallas_call(
        _ada_kernel,
        grid=(N_MOD,),
        in_specs=[_full((SUBLANE, D_MODEL)),
                  pl.BlockSpec((D_MODEL, D_MODEL), lambda j: (0, j)),
                  pl.BlockSpec((1, D_MODEL), lambda j: (0, j))],
        out_specs=pl.BlockSpec((SUBLANE, D_MODEL), lambda j: (0, j)),
        out_shape=jax.ShapeDtypeStruct((SUBLANE, N_MOD * D_MODEL), F32),
        compiler_params=_cparams(("arbitrary",)),
        name="adaln",
    )(cvec, w_ada, b_ada.reshape(1, -1))


def _mod_rows(mod_ref, row, part):
    return mod_ref[pl.ds(row, 1), part * D_MODEL:(part + 1) * D_MODEL]


def _inproj_kernel(*refs, rope, row0, tiles_per_row):
    if rope:
        (x_ref, mod_ref, g_ref, w_ref, cos_ref, sin_ref,
         q_ref, k_ref, v_ref, xr_ref, xg_ref, ga_ref, gr_ref) = refs
    else:
        (x_ref, mod_ref, g_ref, w_ref,
         q_ref, k_ref, v_ref, xr_ref, xg_ref, ga_ref, gr_ref, k32_ref, v32_ref) = refs
    row = row0 + pl.program_id(0) // tiles_per_row
    shift = _mod_rows(mod_ref, row, 0)
    scale = _mod_rows(mod_ref, row, 1)
    h = (_rms(x_ref[...], g_ref[...]) * (1.0 + scale) + shift).astype(BF16)

    def proj(j):
        return jnp.dot(h, w_ref[:, j * D_MODEL:(j + 1) * D_MODEL], preferred_element_type=F32)

    def rotate(x):
        lane = lax.broadcasted_iota(jnp.int32, (x.shape[0], LANE), 1)
        first = (lane & 31) < 16
        outs = []
        for c in range(D_MODEL // LANE):
            xc = x[:, c * LANE:(c + 1) * LANE]
            partner = jnp.where(first, pltpu.roll(xc, LANE - 16, 1), pltpu.roll(xc, 16, 1))
            outs.append(xc * cos_ref[...] + partner * sin_ref[...])
        return jnp.concatenate(outs, axis=1)

    q = proj(0)
    k = proj(1)
    v = proj(2)
    if rope:
        q = rotate(q)
        k = rotate(k)
    else:
        k32_ref[...] = k
        v32_ref[...] = v
    k_ref[...] = k.astype(BF16)
    q = q * Q_SCALE
    for hd in range(N_HEADS):
        cols = slice(hd * V_DIM, (hd + 1) * V_DIM)
        q_ref[0, hd] = q[:, cols].T.astype(BF16)
        v_ref[0, hd, 0, :V_DIM, :] = v[:, cols].T.astype(BF16)
        v_ref[0, hd, 0, V_DIM:, :] = jnp.ones((ONES_ROWS, v.shape[0]), BF16)
    xr_ref[...] = proj(3)
    xg_ref[...] = proj(4)
    ga_ref[...] = proj(5)
    gr_ref[...] = proj(6)


def _inproj(x, mod, g_pre, w_in_bf, *, row0, tokens_per_row, tokens_per_batch, rope_tabs=None):
    t = x.shape[0]
    rope = rope_tabs is not None
    tile = pl.BlockSpec((TM, D_MODEL), lambda i: (i, 0))
    in_specs = [tile, _full(mod.shape), _full((1, D_MODEL)), _resident(w_in_bf.shape)]
    args = [x, mod, g_pre.reshape(1, -1), w_in_bf]
    n_b = t // tokens_per_batch
    tpb = tokens_per_batch // TM
    outs = [jax.ShapeDtypeStruct((n_b, N_HEADS, V_DIM, tokens_per_batch), BF16),
            jax.ShapeDtypeStruct((t, D_MODEL), BF16),
            jax.ShapeDtypeStruct((n_b, N_HEADS, tpb, V_ROWS, TM), BF16)] + [jax.ShapeDtypeStruct((t, D_MODEL), F32)] * 4
    out_specs = [pl.BlockSpec((1, N_HEADS, V_DIM, TM), lambda i: (i // tpb, 0, 0, i % tpb)), tile,
                 pl.BlockSpec((1, N_HEADS, 1, V_ROWS, TM), lambda i: (i // tpb, 0, i % tpb, 0, 0))] + [tile] * 4
    if rope:
        n_pos = rope_tabs[0].shape[0] // TM
        tab = pl.BlockSpec((TM, LANE), lambda i: (i % n_pos, 0))
        in_specs += [tab, tab]
        args += list(rope_tabs)
    else:
        outs += [jax.ShapeDtypeStruct((t, D_MODEL), F32)] * 2
        out_specs += [tile] * 2
    return pl.pallas_call(
        functools.partial(_inproj_kernel, rope=rope, row0=row0, tiles_per_row=tokens_per_row // TM),
        grid=(t // TM,),
        in_specs=in_specs,
        out_specs=out_specs,
        out_shape=outs,
        compiler_params=_cparams(("arbitrary",)),
        name="inproj_lat" if rope else "inproj_ctx",
    )(*args)


def _rope_tables(n_tokens):
    rows = n_tokens // GRID_W
    row = jnp.repeat(jnp.arange(rows, dtype=F32), GRID_W)
    col = jnp.tile(jnp.arange(GRID_W, dtype=F32), rows)
    q4 = HEAD_DIM // 4
    inv_freq = jnp.power(ROPE_THETA, -jnp.arange(q4, dtype=F32) / q4)
    ang = jnp.concatenate([row[:, None] * inv_freq, col[:, None] * inv_freq], axis=-1)
    cos, sin = jnp.cos(ang), jnp.sin(ang)
    c64 = jnp.concatenate([cos[:, :q4], cos[:, :q4], cos[:, q4:], cos[:, q4:]], axis=-1)
    s64 = jnp.concatenate([-sin[:, :q4], sin[:, :q4], -sin[:, q4:], sin[:, q4:]], axis=-1)
    return jnp.tile(c64, (1, LANE // HEAD_DIM)), jnp.tile(s64, (1, LANE // HEAD_DIM))


def _attn_kernel(*refs, tk, n_new, n_cache, lam_init):
    if n_cache:
        lq_ref, lk_ref, gs_ref, q_ref, k_ref, v_ref, kc_ref, vc_ref, o_ref, qs_ref, m_ref, acc_ref = refs
    else:
        lq_ref, lk_ref, gs_ref, q_ref, k_ref, v_ref, o_ref, qs_ref, m_ref, acc_ref = refs
    n_chunks = n_new + n_cache
    qt = q_ref[0, 0]
    tq = qt.shape[1]
    sub = lax.broadcasted_iota(jnp.int32, qt.shape, 0)
    zero = jnp.zeros_like(qt)
    qs_ref[:, :tq] = jnp.where(sub < HEAD_DIM, qt, zero)
    qs_ref[:, tq:] = jnp.where(sub >= HEAD_DIM, qt, zero)
    m_ref[...] = jnp.full(m_ref.shape, -jnp.inf, F32)
    acc_ref[...] = jnp.zeros(acc_ref.shape, F32)

    def scores(c):
        if c < n_new:
            kc = k_ref[0, c * tk:(c + 1) * tk, :]
        else:
            kc = kc_ref[0, (c - n_new) * tk:(c - n_new + 1) * tk, :]
        return jnp.dot(kc, qs_ref[...], preferred_element_type=F32)

    def values(c):
        return v_ref[0, 0, c] if c < n_new else vc_ref[0, 0, c - n_new]

    st = scores(0)
    for c in range(n_chunks):
        st_next = scores(c + 1) if c + 1 < n_chunks else None
        m_old = m_ref[...]
        m_new = jnp.maximum(m_old, jnp.max(st, axis=0, keepdims=True))
        p = jnp.exp2(st - m_new).astype(BF16)
        acc_ref[...] = jnp.exp2(m_old - m_new) * acc_ref[...] + jnp.dot(values(c), p, preferred_element_type=F32)
        m_ref[...] = m_new
        st = st_next

    e = jnp.exp(jnp.sum(lq_ref[...] * lk_ref[...], axis=-1, keepdims=True))
    lam = e[0:1, :] - e[1:2, :] + lam_init
    acc = acc_ref[...]
    den = acc[V_DIM:V_DIM + 1, :]
    ot = acc[:V_DIM, :tq] / den[:, :tq] - lam * (acc[:V_DIM, tq:] / den[:, tq:])
    o_ref[0] = (_rms(ot.T, gs_ref[...]) * (1.0 - lam_init)).astype(o_ref.dtype)


def _cache_layouts(cache_k, cache_v, tk):
    b, p = cache_k.shape[:2]
    kc = cache_k.reshape(b, p, D_MODEL).astype(BF16)
    vt = cache_v.transpose(0, 2, 3, 1)
    vt = jnp.concatenate([vt, jnp.ones((b, N_HEADS, ONES_ROWS, p), vt.dtype)], axis=2)
    return kc, vt.reshape(b, N_HEADS, V_ROWS, p // tk, tk).transpose(0, 1, 3, 2, 4).astype(BF16)


def _attention(qt, k, vt, lam_q, lam_k, g_sub, *, lam_init, tq, cache=None):
    b, _, _, nq = qt.shape
    n = k.shape[1]
    tk = vt.shape[-1]
    n_new = n // tk
    in_specs = [_full((2, HEAD_DIM)), _full((2, HEAD_DIM)), _full((1, V_DIM)),
                pl.BlockSpec((1, 1, V_DIM, tq), lambda bi, h, qi: (bi, h, 0, qi)),
                pl.BlockSpec((1, n, LANE), lambda bi, h, qi: (bi, 0, h)),
                pl.BlockSpec((1, 1, n_new, V_ROWS, tk), lambda bi, h, qi: (bi, h, 0, 0, 0))]
    args = [lam_q, lam_k, g_sub.reshape(1, -1), qt, k, vt]
    n_cache = 0
    if cache is not None:
        kc, vc = cache
        n_cache = vc.shape[2]
        in_specs += [pl.BlockSpec((1, kc.shape[1], LANE), lambda bi, h, qi: (bi, 0, h)),
                     pl.BlockSpec((1, 1, n_cache, V_ROWS, tk), lambda bi, h, qi: (bi, h, 0, 0, 0))]
        args += [kc, vc]
    kern = functools.partial(_attn_kernel, tk=tk, n_new=n_new, n_cache=n_cache, lam_init=lam_init)
    return pl.pallas_call(
        kern,
        grid=(b, N_HEADS, nq // tq),
        in_specs=in_specs,
        out_specs=pl.BlockSpec((1, tq, LANE), lambda bi, h, qi: (bi, qi, h)),
        out_shape=jax.ShapeDtypeStruct((b, nq, D_MODEL), BF16),
        scratch_shapes=[pltpu.VMEM((V_DIM, 2 * tq), BF16), pltpu.VMEM((1, 2 * tq), F32),
                        pltpu.VMEM((V_ROWS, 2 * tq), F32)],
        compiler_params=_cparams(("arbitrary", "arbitrary", "arbitrary")),
        name=f"diff_attn_{n + n_cache * tk}",
    )(*args)


def _sigmoid(x):
    return 0.5 * jnp.tanh(0.5 * x) + 0.5


def _gelu_tanh(x):
    return 0.5 * x * (1.0 + jnp.tanh(math.sqrt(2.0 / math.pi) * (x + 0.044715 * (x * x * x))))


def _rnn_kernel(xr_ref, xg_ref, h0_ref, wc_ref, bc_ref, wg_ref, bg_ref, lam_ref,
                o_ref, hT_ref, xpad_ref, hf_ref, hb_ref, *, n, tt):
    pad = SUBLANE
    cb = xr_ref.shape[2]
    n_chunks = n // tt
    groups = tt // SUBLANE
    xpad_ref[0:pad, :] = jnp.zeros((pad, cb), F32)
    xpad_ref[pad + n:pad + n + pad, :] = jnp.zeros((pad, cb), F32)
    xpad_ref[pad:pad + n, :] = xr_ref[0]
    r8 = lax.broadcasted_iota(jnp.int32, (groups, SUBLANE, cb), 1)

    def gate_inputs(c, d):
        t0 = pl.multiple_of(c * tt, tt)
        blk = xpad_ref[pl.ds(t0, tt + 2 * pad), :]
        ext = tt + 2 * pad
        xc = (bc_ref[...]
              + wc_ref[0:1, :] * pltpu.roll(blk, 2, 0)[pad:pad + tt]
              + wc_ref[1:2, :] * pltpu.roll(blk, 1, 0)[pad:pad + tt]
              + wc_ref[2:3, :] * blk[pad:pad + tt]
              + wc_ref[3:4, :] * pltpu.roll(blk, ext - 1, 0)[pad:pad + tt])
        g = jnp.dot(xc.astype(BF16), wg_ref[d, 0], preferred_element_type=F32)
        r = _sigmoid(g[:, :cb] + bg_ref[d, 0:1, :])
        i = _sigmoid(g[:, cb:] + bg_ref[d, 1:2, :])
        lam = lam_ref[d:d + 1, :]
        softplus = jnp.maximum(-lam, 0.0) + jnp.log1p(jnp.exp(-jnp.abs(lam)))
        log_a = (-LRU_C) * r * softplus
        a = jnp.exp(log_a)
        u = jnp.sqrt(-jnp.tanh(log_a) * (1.0 + a * a)) * (i * xc)
        return t0, a, u

    def scan8(a, u, reverse):
        a = a.reshape(groups, SUBLANE, cb)
        u = u.reshape(groups, SUBLANE, cb)
        for d in (1, 2, 4):
            keep = (r8 < SUBLANE - d) if reverse else (r8 >= d)
            shift = SUBLANE - d if reverse else d
            u = u + jnp.where(keep, a * pltpu.roll(u, shift, 1), 0.0)
            a = jnp.where(keep, a * pltpu.roll(a, shift, 1), a)
        return a, u

    def fwd_chunk(c, hc):
        t0, a, u = gate_inputs(c, 0)
        a, u = scan8(a, u, False)
        for g in range(groups):
            hg = a[g] * hc + u[g]
            hf_ref[pl.ds(t0 + g * SUBLANE, SUBLANE), :] = hg
            hc = hg[SUBLANE - 1:SUBLANE]
        return hc

    def bwd_chunk(j, hc):
        c = n_chunks - 1 - j
        t0, a, u = gate_inputs(c, 1)
        a, u = scan8(a, u, True)
        for g in reversed(range(groups)):
            hg = a[g] * hc + u[g]
            hb_ref[g * SUBLANE:(g + 1) * SUBLANE, :] = hg
            hc = hg[0:1]
        hsum = hf_ref[pl.ds(t0, tt), :] + hb_ref[...]
        o_ref[0, pl.ds(t0, tt), :] = (hsum * _gelu_tanh(xg_ref[0, pl.ds(t0, tt), :])).astype(o_ref.dtype)
        return hc

    h_fwd = lax.fori_loop(0, n_chunks, fwd_chunk, h0_ref[0, 0:1, :])
    h_bwd = lax.fori_loop(0, n_chunks, bwd_chunk, h0_ref[0, 1:2, :])
    hT_ref[0, 0:1, :] = h_fwd
    hT_ref[0, 1:2, :] = h_bwd


def _rglru(xr, xg, h0, w_conv, b_conv, wg_bd, b_gate, lam):
    b, n, _ = xr.shape
    tt = min(RNN_TT, n)
    n_cb = D_RNN // RNN_CB
    slab = pl.BlockSpec((1, n, RNN_CB), lambda bi, ci: (bi, 0, ci))
    state = pl.BlockSpec((1, 2, RNN_CB), lambda bi, ci: (bi, 0, ci))
    return pl.pallas_call(
        functools.partial(_rnn_kernel, n=n, tt=tt),
        grid=(b, n_cb),
        in_specs=[slab, slab, state,
                  pl.BlockSpec((CONV_W, RNN_CB), lambda bi, ci: (0, ci)),
                  pl.BlockSpec((1, RNN_CB), lambda bi, ci: (0, ci)),
                  pl.BlockSpec((2, 1, RNN_CB, 2 * RNN_CB), lambda bi, ci: (0, ci, 0, 0)),
                  pl.BlockSpec((2, 2, RNN_CB), lambda bi, ci: (0, 0, ci)),
                  pl.BlockSpec((2, RNN_CB), lambda bi, ci: (0, ci))],
        out_specs=[slab, state],
        out_shape=[jax.ShapeDtypeStruct((b, n, D_RNN), BF16), jax.ShapeDtypeStruct((b, 2, D_RNN), F32)],
        scratch_shapes=[pltpu.VMEM((n + 2 * SUBLANE, RNN_CB), F32), pltpu.VMEM((n, RNN_CB), F32),
                        pltpu.VMEM((tt, RNN_CB), F32)],
        compiler_params=_cparams(("arbitrary", "arbitrary")),
        name=f"rglru_{n}",
    )(xr, xg, h0, w_conv, b_conv.reshape(1, -1), wg_bd, b_gate, lam)


def _gate_weights(w_gate):
    per = RNN_CB // RNN_BLOCK_DIM
    n_cb = D_RNN // RNN_CB
    w = w_gate.reshape(2, 2, n_cb, per, RNN_BLOCK_DIM, RNN_BLOCK_DIM)
    eye = jnp.eye(per, dtype=w.dtype)
    bd = jnp.einsum('dkcpij,pq->dkcpiqj', w, eye).reshape(2, 2, n_cb, RNN_CB, RNN_CB)
    return jnp.concatenate([bd[:, 0], bd[:, 1]], axis=-1).astype(BF16)


def _merge_kernel(oa_ref, or_ref, ga_ref, gr_ref, x_ref, mod_ref, base_ref, gpost_ref, gpre_ref,
                  wa_ref, wr_ref, wo_ref, wrh_ref, wrl_ref, br_ref,
                  y_ref, h_ref, info_ref, gate_ref, cnt_ref, run_ref, *, row0, tiles_per_row):
    i = pl.program_id(0)
    row = row0 + i // tiles_per_row
    ya = jnp.dot(oa_ref[...], wa_ref[...], preferred_element_type=F32)
    yr = jnp.dot(or_ref[...], wr_ref[...], preferred_element_type=F32)
    y = _sigmoid(ga_ref[...]) * ya + _sigmoid(gr_ref[...]) * yr
    mix = jnp.dot(y.astype(BF16), wo_ref[...], preferred_element_type=F32)
    y1 = x_ref[...] + _mod_rows(mod_ref, row, 2) * _rms(mix, gpost_ref[...])
    y_ref[...] = y1
    h = _rms(y1, gpre_ref[...]) * (1.0 + _mod_rows(mod_ref, row, 4)) + _mod_rows(mod_ref, row, 3)
    h_ref[...] = h

    h_hi = h.astype(BF16)
    h_lo = (h - h_hi.astype(F32)).astype(BF16)
    logits = (jnp.dot(h_hi, wrh_ref[...], preferred_element_type=F32)
              + jnp.dot(h_lo, wrh_ref[...], preferred_element_type=F32)
              + jnp.dot(h_hi, wrl_ref[...], preferred_element_type=F32)) + br_ref[...]
    tm = logits.shape[0]
    lane = lax.broadcasted_iota(jnp.int32, (tm, LANE), 1)
    work = logits
    sel = []
    vals = []
    for _ in range(TOP_K):
        mx = jnp.max(work, axis=-1, keepdims=True)
        idx = jnp.min(jnp.where(work == mx, lane, LANE), axis=-1, keepdims=True)
        hit = lane == idx
        sel.append((idx, hit))
        vals.append(mx)
        work = jnp.where(hit, NEG_BIG * 2.0, work)
    exps = [jnp.exp(v - vals[0]) for v in vals]
    inv = 1.0 / (exps[0] + exps[1] + exps[2] + exps[3])

    @pl.when(i == 0)
    def _():
        run_ref[...] = base_ref[...]

    mask = jnp.zeros((tm, LANE), F32)
    for _, hit in sel:
        mask = mask + hit.astype(F32)
    rr = lax.broadcasted_iota(jnp.int32, (tm, tm), 0)
    cc = lax.broadcasted_iota(jnp.int32, (tm, tm), 1)
    tri = (cc < rr).astype(BF16)
    rank = jnp.dot(tri, mask.astype(BF16), preferred_element_type=F32) + run_ref[...]
    run_ref[...] = run_ref[...] + jnp.sum(mask, axis=0, keepdims=True)
    cnt_ref[...] = run_ref[...]

    info = jnp.zeros((tm, LANE), jnp.int32)
    gates = jnp.zeros((tm, LANE), F32)
    for kk, (idx, hit) in enumerate(sel):
        rk = jnp.sum(jnp.where(hit, rank, 0.0), axis=-1, keepdims=True).astype(jnp.int32)
        info = jnp.where(lane == kk, idx, info)
        info = jnp.where(lane == TOP_K + kk, rk, info)
        gates = jnp.where(lane == kk, exps[kk] * inv, gates)
    info_ref[...] = info
    gate_ref[...] = gates


def _merge(oa, orec, ga, gr, x, mod, base, g_post, g_pre, wa, wr, wo, wr_hi, wr_lo, b_router,
           *, row0, tokens_per_row):
    t = x.shape[0]
    tile = pl.BlockSpec((TM, D_MODEL), lambda i: (i, 0))
    small = pl.BlockSpec((TM, LANE), lambda i: (i, 0))
    vec = _full((1, D_MODEL))
    wspec = _resident((D_MODEL, D_MODEL))
    rspec = _resident((D_MODEL, LANE))
    return pl.pallas_call(
        functools.partial(_merge_kernel, row0=row0, tiles_per_row=tokens_per_row // TM),
        grid=(t // TM,),
        in_specs=[tile, tile, tile, tile, tile, _full(mod.shape), _full((1, LANE)), vec, vec,
                  wspec, wspec, wspec, rspec, rspec, _full((1, LANE))],
        out_specs=[tile, tile, small, small, _full((1, LANE))],
        out_shape=[jax.ShapeDtypeStruct((t, D_MODEL), F32), jax.ShapeDtypeStruct((t, D_MODEL), F32),
                   jax.ShapeDtypeStruct((t, LANE), jnp.int32), jax.ShapeDtypeStruct((t, LANE), F32),
                   jax.ShapeDtypeStruct((1, LANE), F32)],
        scratch_shapes=[pltpu.VMEM((1, LANE), F32)],
        compiler_params=_cparams(("arbitrary",)),
        name=f"merge_router_{t}",
    )(oa, orec, ga, gr, x, mod, base, g_post.reshape(1, -1), g_pre.reshape(1, -1),
      wa, wr, wo, wr_hi, wr_lo, b_router)


TD = 128


SEL_W = 256


def _expert_kernel(be_ref, nb_ref, par_ref, nxt_ref, tokc_ref, tokn_ref, h_ref, wu_ref, bu_ref, wd_ref, bd_ref,
                   ys_ref, wuf_ref, wdf_ref, wub_ref, wdb_ref, xbuf_ref, wsem, xsem):
    i = pl.program_id(0)
    nb = nb_ref[0]
    e = be_ref[i]
    par = par_ref[i]
    slot = i % 2

    def weight_copies(expert, p):
        return (pltpu.make_async_copy(wu_ref.at[expert], wuf_ref.at[p], wsem.at[0, p]),
                pltpu.make_async_copy(wd_ref.at[expert], wdf_ref.at[p], wsem.at[1, p]))

    def row_copy(tok_ref, r, s):
        return pltpu.make_async_copy(h_ref.at[pl.ds(tok_ref[r], 1), :], xbuf_ref.at[s, pl.ds(r, 1), :], xsem.at[s])

    @pl.when(i == 0)
    def _():
        for cp in weight_copies(e, par):
            cp.start()
        for r in range(MOE_BLOCK):
            row_copy(tokc_ref, r, slot).start()

    new_expert = (i == 0) | (e != be_ref[jnp.maximum(i - 1, 0)])

    @pl.when((i < nb) & new_expert)
    def _():
        for cp in weight_copies(e, par):
            cp.wait()

        @pl.when(nxt_ref[i] != e)
        def _():
            for cp in weight_copies(nxt_ref[i], 1 - par):
                cp.start()

        r = lax.broadcasted_iota(jnp.int32, (SEL_W, SEL_W), 0)
        c = lax.broadcasted_iota(jnp.int32, (SEL_W, SEL_W), 1)
        half = SEL_W // 2
        sel = jnp.where(r == jnp.where(c < half, 2 * c, 2 * (c - half) + 1), 1.0, 0.0).astype(BF16)
        for g in range(2 * D_FF // SEL_W):
            cols = wuf_ref[par, :, g * SEL_W:(g + 1) * SEL_W].astype(BF16)
            d = jnp.dot(cols, sel, preferred_element_type=F32).astype(BF16)
            wub_ref[:, g * half:(g + 1) * half] = d[:, :half]
            wub_ref[:, D_FF + g * half:D_FF + (g + 1) * half] = d[:, half:]
        wdb_ref[...] = wdf_ref[par].astype(BF16)

    @pl.when(i < nb)
    def _():
        for r in range(MOE_BLOCK):
            row_copy(tokc_ref, r, slot).wait()
        hu = jnp.dot(xbuf_ref[slot].astype(BF16), wub_ref[...], preferred_element_type=F32) + bu_ref[0]
        for r in range(MOE_BLOCK):
            row_copy(tokn_ref, r, 1 - slot).start()
        glu = jnp.minimum(hu[:, :D_FF], SWIGLU_LIMIT)
        lin = jnp.clip(hu[:, D_FF:], -SWIGLU_LIMIT, SWIGLU_LIMIT)
        act = (lin + 1.0) * glu * _sigmoid(SWIGLU_ALPHA * glu)
        ys_ref[...] = jnp.dot(act.astype(BF16), wdb_ref[...], preferred_element_type=F32) + bd_ref[0]

    @pl.when(i == nb - 1)
    def _():
        for r in range(MOE_BLOCK):
            row_copy(tokn_ref, r, 1 - slot).wait()

    @pl.when(i >= nb)
    def _():
        ys_ref[...] = jnp.zeros(ys_ref.shape, F32)


def _experts(blk_e, n_used, run_par, next_e, slot_tok, h_all, wu, bu, wd, bd):
    ns = slot_tok.shape[0]
    hbm = pl.BlockSpec(memory_space=pl.ANY)
    grid_spec = pltpu.PrefetchScalarGridSpec(
        num_scalar_prefetch=4,
        grid=(ns // MOE_BLOCK,),
        in_specs=[pl.BlockSpec((MOE_BLOCK,), lambda i, be, nb, pr, nx: (jnp.minimum(i, nb[0] - 1),),
                               memory_space=pltpu.SMEM),
                  pl.BlockSpec((MOE_BLOCK,), lambda i, be, nb, pr, nx: (jnp.minimum(i + 1, nb[0] - 1),),
                               memory_space=pltpu.SMEM),
                  hbm, hbm,
                  pl.BlockSpec((1, 1, 2 * D_FF), lambda i, be, nb, pr, nx: (be[i], 0, 0)),
                  hbm,
                  pl.BlockSpec((1, 1, D_MODEL), lambda i, be, nb, pr, nx: (be[i], 0, 0))],
        out_specs=pl.BlockSpec((MOE_BLOCK, D_MODEL), lambda i, be, nb, pr, nx: (i, 0)),
        scratch_shapes=[pltpu.VMEM((2, D_MODEL, 2 * D_FF), F32), pltpu.VMEM((2, D_FF, D_MODEL), F32),
                        pltpu.VMEM((D_MODEL, 2 * D_FF), BF16), pltpu.VMEM((D_FF, D_MODEL), BF16),
                        pltpu.VMEM((2, MOE_BLOCK, D_MODEL), F32),
                        pltpu.SemaphoreType.DMA((2, 2)), pltpu.SemaphoreType.DMA((2,))],
    )
    return pl.pallas_call(
        _expert_kernel,
        grid_spec=grid_spec,
        out_shape=jax.ShapeDtypeStruct((ns, D_MODEL), F32),
        compiler_params=_cparams(("arbitrary",)),
        name="moe_experts",
    )(blk_e, n_used, run_par, next_e, slot_tok, slot_tok, h_all, wu, bu, wd, bd)


def _combine_kernel(dest_ref, ys_ref, gate_ref, y1_ref, mod_ref, gpost_ref, o_ref, buf_ref, sem,
                    *, row0, tiles_per_row):
    row = row0 + pl.program_id(0) // tiles_per_row

    def copy(r, k):
        return pltpu.make_async_copy(ys_ref.at[pl.ds(dest_ref[r * TOP_K + k], 1), :],
                                     buf_ref.at[k, pl.ds(r, 1), :], sem)

    def start(r, c):
        for k in range(TOP_K):
            copy(r, k).start()
        return c

    def wait(r, c):
        for k in range(TOP_K):
            copy(r, k).wait()
        return c

    lax.fori_loop(0, TD, start, 0)
    lax.fori_loop(0, TD, wait, 0)
    g = gate_ref[...]
    moe = g[:, 0:1] * buf_ref[0]
    for kk in range(1, TOP_K):
        moe = moe + g[:, kk:kk + 1] * buf_ref[kk]
    o_ref[...] = y1_ref[...] + _mod_rows(mod_ref, row, 5) * _rms(moe, gpost_ref[...])


def _combine(dest_flat, ys, gates, y1, mod, g_post, *, row0, tokens_per_row):
    t = y1.shape[0]
    tile = pl.BlockSpec((TD, D_MODEL), lambda i: (i, 0))
    return pl.pallas_call(
        functools.partial(_combine_kernel, row0=row0, tiles_per_row=tokens_per_row // TD),
        grid=(t // TD,),
        in_specs=[pl.BlockSpec((TD * TOP_K,), lambda i: (i,), memory_space=pltpu.SMEM),
                  pl.BlockSpec(memory_space=pl.ANY),
                  pl.BlockSpec((TD, LANE), lambda i: (i, 0)), tile, _full(mod.shape), _full((1, D_MODEL))],
        out_specs=tile,
        out_shape=jax.ShapeDtypeStruct((t, D_MODEL), F32),
        scratch_shapes=[pltpu.VMEM((TOP_K, TD, D_MODEL), F32), pltpu.SemaphoreType.DMA],
        compiler_params=_cparams(("arbitrary",)),
        name=f"moe_combine_{t}",
    )(dest_flat, ys, gates, y1, mod, g_post.reshape(1, -1))


def kernel(x_prompt, x_sample, cache_k, cache_v, state_h, c, c_ctx, w_ada, b_ada, g_pre_mix, g_post_mix, g_pre_ffn, g_post_ffn, w_in, lam_q, lam_k, g_subln, w_conv, b_conv, w_lru_gate, b_lru_gate, lru_lambda, w_attn_proj, w_rec_proj, w_out, w_router, b_router, w_up, b_up, w_down, b_down):
    depth = w_in.shape[0]
    bp, sp, _ = x_prompt.shape
    bs, ss, _ = x_sample.shape
    tp, ts = bp * sp, bs * ss
    cos_t, sin_t = _rope_tables(ss)

    y_p = x_prompt.reshape(tp, D_MODEL)
    y_s = x_sample.reshape(ts, D_MODEL)
    ks, vs, hs = [], [], []
    for l in range(depth):
        lam_init = 0.8 - 0.6 * math.exp(-0.3 * l)
        cvec = jnp.zeros((SUBLANE, D_MODEL), F32).at[0].set(c_ctx).at[1:1 + bs].set(c)
        mod = _adaln(cvec, w_ada[l], b_ada[l])
        w_in_bf = w_in[l].astype(BF16)
        wa, wr, wo = w_attn_proj[l].astype(BF16), w_rec_proj[l].astype(BF16), w_out[l].astype(BF16)
        wg_bd = _gate_weights(w_lru_gate[l])
        w_rt = jnp.pad(w_router[l], ((0, 0), (0, LANE - N_EXPERTS)))
        wr_hi = w_rt.astype(BF16)
        wr_lo = (w_rt - wr_hi.astype(F32)).astype(BF16)
        b_rt = jnp.pad(b_router[l], (0, LANE - N_EXPERTS), constant_values=NEG_BIG).reshape(1, LANE)
        bu = b_up[l].reshape(N_EXPERTS, D_FF, 2)
        bu = jnp.concatenate([bu[..., 0], bu[..., 1]], axis=-1).reshape(N_EXPERTS, 1, 2 * D_FF)
        bd = b_down[l].reshape(N_EXPERTS, 1, D_MODEL)

        qt, k, vt, xr, xg, ga, gr, k32, v32 = _inproj(y_p, mod, g_pre_mix[l], w_in_bf, row0=0, tokens_per_row=tp,
                                                      tokens_per_batch=sp)
        oa = _attention(qt, k.reshape(bp, sp, D_MODEL), vt, lam_q[l], lam_k[l], g_subln[l],
                        lam_init=lam_init, tq=sp)
        orec, h_t = _rglru(xr.reshape(bp, sp, D_RNN), xg.reshape(bp, sp, D_RNN),
                           jnp.zeros((bp, 2, D_RNN), F32), w_conv[l], b_conv[l], wg_bd, b_lru_gate[l], lru_lambda[l])
        ks.append(k32.reshape(bp, sp, N_HEADS, 2, HEAD_DIM))
        vs.append(v32.reshape(bp, sp, N_HEADS, V_DIM))
        hs.append(h_t)
        y1_p, h2_p, info_p, gate_p, cnt_p = _merge(
            oa.reshape(tp, D_MODEL), orec.reshape(tp, D_MODEL), ga, gr, y_p, mod, jnp.zeros((1, LANE), F32),
            g_post_mix[l], g_pre_ffn[l], wa, wr, wo, wr_hi, wr_lo, b_rt, row0=0, tokens_per_row=tp)

        qt, k, vt, xr, xg, ga, gr = _inproj(y_s, mod, g_pre_mix[l], w_in_bf, row0=1, tokens_per_row=ss,
                                            tokens_per_batch=ss, rope_tabs=(cos_t, sin_t))
        oa = _attention(qt, k.reshape(bs, ss, D_MODEL), vt, lam_q[l], lam_k[l], g_subln[l],
                        lam_init=lam_init, tq=512, cache=_cache_layouts(cache_k[:, l], cache_v[:, l], TM))
        orec, _ = _rglru(xr.reshape(bs, ss, D_RNN), xg.reshape(bs, ss, D_RNN), state_h[:, l],
                         w_conv[l], b_conv[l], wg_bd, b_lru_gate[l], lru_lambda[l])
        y1_s, h2_s, info_s, gate_s, cnt_s = _merge(
            oa.reshape(ts, D_MODEL), orec.reshape(ts, D_MODEL), ga, gr, y_s, mod, cnt_p,
            g_post_mix[l], g_pre_ffn[l], wa, wr, wo, wr_hi, wr_lo, b_rt, row0=1, tokens_per_row=ss)

        counts = cnt_s[0, :N_EXPERTS].astype(jnp.int32)
        padded = (counts + MOE_BLOCK - 1) // MOE_BLOCK * MOE_BLOCK
        pad_end = jnp.cumsum(padded)
        pad_start = pad_end - padded
        n_slots = (tp + ts) * TOP_K + N_EXPERTS * MOE_BLOCK
        n_blocks = n_slots // MOE_BLOCK
        blk_start = jnp.arange(n_blocks, dtype=jnp.int32) * MOE_BLOCK
        blk_e = jnp.minimum(jnp.sum(pad_end[None, :] <= blk_start[:, None], axis=1), N_EXPERTS - 1).astype(jnp.int32)
        n_used = (pad_end[-1:] // MOE_BLOCK).astype(jnp.int32)

        def dest_of(info):
            e = info[:, :TOP_K]
            onehot = e[:, :, None] == jnp.arange(N_EXPERTS, dtype=jnp.int32)[None, None, :]
            return (jnp.sum(jnp.where(onehot, pad_start[None, None, :], 0), axis=-1)
                    + info[:, TOP_K:2 * TOP_K]).reshape(-1)

        dest_p, dest_s = dest_of(info_p), dest_of(info_s)
        tok_ids = jnp.repeat(jnp.arange(tp + ts, dtype=jnp.int32), TOP_K)
        slot_tok = jnp.zeros((n_slots,), jnp.int32).at[jnp.concatenate([dest_p, dest_s])].set(
            tok_ids, unique_indices=True)
        run_par = (jnp.cumsum(jnp.concatenate([jnp.zeros((1,), jnp.int32),
                                               (blk_e[1:] != blk_e[:-1]).astype(jnp.int32)])) % 2).astype(jnp.int32)
        ids = jnp.arange(N_EXPERTS, dtype=jnp.int32)
        later = lax.cummin(jnp.where(padded > 0, ids, N_EXPERTS), reverse=True)
        next_active = jnp.concatenate([later[1:], jnp.full((1,), N_EXPERTS, jnp.int32)])
        next_active = jnp.where(next_active < N_EXPERTS, next_active, ids)
        next_e = jnp.sum(jnp.where(blk_e[:, None] == ids[None, :], next_active[None, :], 0), axis=1).astype(jnp.int32)
        ys = _experts(blk_e, n_used, run_par, next_e, slot_tok, jnp.concatenate([h2_p, h2_s], axis=0),
                      w_up[l], bu, w_down[l], bd)
        y_p = _combine(dest_p, ys, gate_p, y1_p, mod, g_post_ffn[l], row0=0, tokens_per_row=tp)
        y_s = _combine(dest_s, ys, gate_s, y1_s, mod, g_post_ffn[l], row0=1, tokens_per_row=ss)

    return (y_p.reshape(bp, sp, D_MODEL), y_s.reshape(bs, ss, D_MODEL),
            jnp.stack(ks, axis=1), jnp.stack(vs, axis=1), jnp.stack(hs, axis=1))
```

```python
import functools
import math

import jax
import jax.numpy as jnp
from jax import lax
from jax.experimental import pallas as pl
from jax.experimental.pallas import tpu as pltpu

F32 = jnp.float32
BF16 = jnp.bfloat16

D_MODEL = 1024
N_HEADS = 8
HEAD_DIM = 64
V_DIM = 2 * HEAD_DIM
GRID_W = 64
D_RNN = D_MODEL
RNN_BLOCKS = 16
RNN_BLOCK_DIM = D_RNN // RNN_BLOCKS
CONV_W = 4
LRU_C = 8.0
N_EXPERTS = 32
TOP_K = 4
D_FF = D_MODEL
SWIGLU_ALPHA = 1.702
SWIGLU_LIMIT = 7.0
ROPE_THETA = 10000.0
MOE_BLOCK = 256
EPS = 1e-6
N_MOD = 6
IN_PARTS = 7

LANE = 128
SUBLANE = 8
VMEM_LIMIT = 56 * 1024 * 1024

TM = 256
RNN_CB = 256
RNN_TT = 256
NEG_BIG = -1e30
ATTN_COLS = 1024
ONES_ROWS = 16
V_ROWS = V_DIM + ONES_ROWS
Q_SCALE =HEAD_DIM ** -0.5 * math.log2(math.e)


def _cparams(sem):
    return pltpu.CompilerParams(dimension_semantics=sem, vmem_limit_bytes=VMEM_LIMIT)


def _full(shape):
    return pl.BlockSpec(shape, lambda *_: (0,) * len(shape))


def _resident(shape):
    return pl.BlockSpec(shape, lambda *_: (0,) * len(shape), pipeline_mode=pl.Buffered(1))


def _rms(x, g):
    return x * lax.rsqrt(jnp.mean(x * x, axis=-1, keepdims=True) + EPS) * g


def _ada_kernel(c_ref, w_ref, b_ref, o_ref):
    c = c_ref[...]
    s = c * jax.nn.sigmoid(c)
    o_ref[...] = jnp.dot(s.astype(BF16), w_ref[...].astype(BF16), preferred_element_type=F32) + b_ref[...]


def _adaln(cvec, w_ada, b_ada):
    return pl.pallas_call(
        _ada_kernel,
        grid=(N_MOD,),
        in_specs=[_full((SUBLANE, D_MODEL)),
                  pl.BlockSpec((D_MODEL, D_MODEL), lambda j: (0, j)),
                  pl.BlockSpec((1, D_MODEL), lambda j: (0, j))],
        out_specs=pl.BlockSpec((SUBLANE, D_MODEL), lambda j: (0, j)),
        out_shape=jax.ShapeDtypeStruct((SUBLANE, N_MOD * D_MODEL), F32),
        compiler_params=_cparams(("arbitrary",)),
        name="adaln",
    )(cvec, w_ada, b_ada.reshape(1, -1))


def _mod_rows(mod_ref, row, part):
    return mod_ref[pl.ds(row, 1), part * D_MODEL:(part + 1) * D_MODEL]


def _inproj_kernel(*refs, rope, row0, tiles_per_row):
    if rope:
        (x_ref, mod_ref, g_ref, w_ref, cos_ref, sin_ref,
         q_ref, k_ref, v_ref, xr_ref, xg_ref, ga_ref, gr_ref) = refs
    else:
        (x_ref, mod_ref, g_ref, w_ref,
         q_ref, k_ref, v_ref, xr_ref, xg_ref, ga_ref, gr_ref, k32_ref, v32_ref) = refs
    row = row0 + pl.program_id(0) // tiles_per_row
    shift = _mod_rows(mod_ref, row, 0)
    scale = _mod_rows(mod_ref, row, 1)
    h = (_rms(x_ref[...], g_ref[...]) * (1.0 + scale) + shift).astype(BF16)

    def proj(j):
        return jnp.dot(h, w_ref[:, j * D_MODEL:(j + 1) * D_MODEL], preferred_element_type=F32)

    def rotate(x):
        lane = lax.broadcasted_iota(jnp.int32, (x.shape[0], LANE), 1)
        first = (lane & 31) < 16
        outs = []
        for c in range(D_MODEL // LANE):
            xc = x[:, c * LANE:(c + 1) * LANE]
            partner = jnp.where(first, pltpu.roll(xc, LANE - 16, 1), pltpu.roll(xc, 16, 1))
            outs.append(xc * cos_ref[...] + partner * sin_ref[...])
        return jnp.concatenate(outs, axis=1)

    q = proj(0)
    k = proj(1)
    v = proj(2)
    if rope:
        q = rotate(q)
        k = rotate(k)
    else:
        k32_ref[...] = k
        v32_ref[...] = v
    k_ref[...] = k.astype(BF16)
    q = q * Q_SCALE
    for hd in range(N_HEADS):
        cols = slice(hd * V_DIM, (hd + 1) * V_DIM)
        q_ref[0, hd] = q[:, cols].T.astype(BF16)
        v_ref[0, hd, 0, :V_DIM, :] = v[:, cols].T.astype(BF16)
        v_ref[0, hd, 0, V_DIM:, :] = jnp.ones((ONES_ROWS, v.shape[0]), BF16)
    xr_ref[...] = proj(3)
    xg_ref[...] = proj(4)
    ga_ref[...] = proj(5)
    gr_ref[...] = proj(6)


def _inproj(x, mod, g_pre, w_in_bf, *, row0, tokens_per_row, tokens_per_batch, rope_tabs=None):
    t = x.shape[0]
    rope = rope_tabs is not None
    tile = pl.BlockSpec((TM, D_MODEL), lambda i: (i, 0))
    in_specs = [tile, _full(mod.shape), _full((1, D_MODEL)), _resident(w_in_bf.shape)]
    args = [x, mod, g_pre.reshape(1, -1), w_in_bf]
    n_b = t // tokens_per_batch
    tpb = tokens_per_batch // TM
    outs = [jax.ShapeDtypeStruct((n_b, N_HEADS, V_DIM, tokens_per_batch), BF16),
            jax.ShapeDtypeStruct((t, D_MODEL), BF16),
            jax.ShapeDtypeStruct((n_b, N_HEADS, tpb, V_ROWS, TM), BF16)] + [jax.ShapeDtypeStruct((t, D_MODEL), F32)] * 4
    out_specs = [pl.BlockSpec((1, N_HEADS, V_DIM, TM), lambda i: (i // tpb, 0, 0, i % tpb)), tile,
                 pl.BlockSpec((1, N_HEADS, 1, V_ROWS, TM), lambda i: (i // tpb, 0, i % tpb, 0, 0))] + [tile] * 4
    if rope:
        n_pos = rope_tabs[0].shape[0] // TM
        tab = pl.BlockSpec((TM, LANE), lambda i: (i % n_pos, 0))
        in_specs += [tab, tab]
        args += list(rope_tabs)
    else:
        outs += [jax.ShapeDtypeStruct((t, D_MODEL), F32)] * 2
        out_specs += [tile] * 2
    return pl.pallas_call(
        functools.partial(_inproj_kernel, rope=rope, row0=row0, tiles_per_row=tokens_per_row // TM),
        grid=(t // TM,),
        in_specs=in_specs,
        out_specs=out_specs,
        out_shape=outs,
        compiler_params=_cparams(("arbitrary",)),
        name="inproj_lat" if rope else "inproj_ctx",
    )(*args)


def _rope_tables(n_tokens):
    rows = n_tokens // GRID_W
    row = jnp.repeat(jnp.arange(rows, dtype=F32), GRID_W)
    col = jnp.tile(jnp.arange(GRID_W, dtype=F32), rows)
    q4 = HEAD_DIM // 4
    inv_freq = jnp.power(ROPE_THETA, -jnp.arange(q4, dtype=F32) / q4)
    ang = jnp.concatenate([row[:, None] * inv_freq, col[:, None] * inv_freq], axis=-1)
    cos, sin = jnp.cos(ang), jnp.sin(ang)
    c64 = jnp.concatenate([cos[:, :q4], cos[:, :q4], cos[:, q4:], cos[:, q4:]], axis=-1)
    s64 = jnp.concatenate([-sin[:, :q4], sin[:, :q4], -sin[:, q4:], sin[:, q4:]], axis=-1)
    return jnp.tile(c64, (1, LANE // HEAD_DIM)), jnp.tile(s64, (1, LANE // HEAD_DIM))


def _attn_kernel(*refs, tk, n_new, n_cache, lam_init):
    if n_cache:
        lq_ref, lk_ref, gs_ref, q_ref, k_ref, v_ref, kc_ref, vc_ref, o_ref, qs_ref, m_ref, acc_ref = refs
    else:
        lq_ref, lk_ref, gs_ref, q_ref, k_ref, v_ref, o_ref, qs_ref, m_ref, acc_ref = refs
    n_chunks = n_new + n_cache
    qt = q_ref[0, 0]
    tq = qt.shape[1]
    sub = lax.broadcasted_iota(jnp.int32, qt.shape, 0)
    zero = jnp.zeros_like(qt)
    qs_ref[:, :tq] = jnp.where(sub < HEAD_DIM, qt, zero)
    qs_ref[:, tq:] = jnp.where(sub >= HEAD_DIM, qt, zero)
    m_ref[...] = jnp.full(m_ref.shape, -jnp.inf, F32)
    acc_ref[...] = jnp.zeros(acc_ref.shape, F32)

    def scores(item):
        c, cols = item
        if c < n_new:
            kc = k_ref[0, c * tk:(c + 1) * tk, :]
        else:
            kc = kc_ref[0, (c - n_new) * tk:(c - n_new + 1) * tk, :]
        return jnp.dot(kc, qs_ref[:, cols], preferred_element_type=F32)

    def values(c):
        return v_ref[0, 0, c] if c < n_new else vc_ref[0, 0, c - n_new]

    width = min(ATTN_COLS, 2 * tq)
    items = [(c, slice(j * width, (j + 1) * width)) for c in range(n_chunks) for j in range(2 * tq // width)]
    st = scores(items[0])
    for n, (c, cols) in enumerate(items):
        st_next = scores(items[n + 1]) if n + 1 < len(items) else None
        m_old = m_ref[:, cols]
        m_new = jnp.maximum(m_old, jnp.max(st, axis=0, keepdims=True))
        p = jnp.exp2(st - m_new).astype(BF16)
        acc_ref[:, cols] = (jnp.exp2(m_old - m_new) * acc_ref[:, cols]
                            + jnp.dot(values(c), p, preferred_element_type=F32))
        m_ref[:, cols] = m_new
        st = st_next

    e = jnp.exp(jnp.sum(lq_ref[...] * lk_ref[...], axis=-1, keepdims=True))
    lam = e[0:1, :] - e[1:2, :] + lam_init
    acc = acc_ref[...]
    den = acc[V_DIM:V_DIM + 1, :]
    ot = acc[:V_DIM, :tq] / den[:, :tq] - lam * (acc[:V_DIM, tq:] / den[:, tq:])
    o_ref[0] = (_rms(ot.T, gs_ref[...]) * (1.0 - lam_init)).astype(o_ref.dtype)


def _cache_layouts(cache_k, cache_v, tk):
    b, p = cache_k.shape[:2]
    kc = cache_k.reshape(b, p, D_MODEL).astype(BF16)
    vt = cache_v.transpose(0, 2, 3, 1)
    vt = jnp.concatenate([vt, jnp.ones((b, N_HEADS, ONES_ROWS, p), vt.dtype)], axis=2)
    return kc, vt.reshape(b, N_HEADS, V_ROWS, p // tk, tk).transpose(0, 1, 3, 2, 4).astype(BF16)


def _attention(qt, k, vt, lam_q, lam_k, g_sub, *, lam_init, tq, cache=None):
    b, _, _, nq = qt.shape
    n = k.shape[1]
    tk = vt.shape[-1]
    n_new = n // tk
    in_specs = [_full((2, HEAD_DIM)), _full((2, HEAD_DIM)), _full((1, V_DIM)),
                pl.BlockSpec((1, 1, V_DIM, tq), lambda bi, h, qi: (bi, h, 0, qi)),
                pl.BlockSpec((1, n, LANE), lambda bi, h, qi: (bi, 0, h)),
                pl.BlockSpec((1, 1, n_new, V_ROWS, tk), lambda bi, h, qi: (bi, h, 0, 0, 0))]
    args = [lam_q, lam_k, g_sub.reshape(1, -1), qt, k, vt]
    n_cache = 0
    if cache is not None:
        kc, vc = cache
        n_cache = vc.shape[2]
        in_specs += [pl.BlockSpec((1, kc.shape[1], LANE), lambda bi, h, qi: (bi, 0, h)),
                     pl.BlockSpec((1, 1, n_cache, V_ROWS, tk), lambda bi, h, qi: (bi, h, 0, 0, 0))]
        args += [kc, vc]
    kern = functools.partial(_attn_kernel, tk=tk, n_new=n_new, n_cache=n_cache, lam_init=lam_init)
    return pl.pallas_call(
        kern,
        grid=(b, N_HEADS, nq // tq),
        in_specs=in_specs,
        out_specs=pl.BlockSpec((1, tq, LANE), lambda bi, h, qi: (bi, qi, h)),
        out_shape=jax.ShapeDtypeStruct((b, nq, D_MODEL), BF16),
        scratch_shapes=[pltpu.VMEM((V_DIM, 2 * tq), BF16), pltpu.VMEM((1, 2 * tq), F32),
                        pltpu.VMEM((V_ROWS, 2 * tq), F32)],
        compiler_params=_cparams(("arbitrary", "arbitrary", "arbitrary")),
        name=f"diff_attn_{n + n_cache * tk}",
    )(*args)


def _sigmoid(x):
    return 0.5 * jnp.tanh(0.5 * x) + 0.5


def _gelu_tanh(x):
    return 0.5 * x * (1.0 + jnp.tanh(math.sqrt(2.0 / math.pi) * (x + 0.044715 * (x * x * x))))


def _rnn_kernel(xr_ref, xg_ref, h0_ref, wc_ref, bc_ref, wg_ref, bg_ref, lam_ref,
                o_ref, hT_ref, xpad_ref, hf_ref, hb_ref, *, n, tt):
    pad = SUBLANE
    cb = xr_ref.shape[2]
    n_chunks = n // tt
    groups = tt // SUBLANE
    xpad_ref[0:pad, :] = jnp.zeros((pad, cb), F32)
    xpad_ref[pad + n:pad + n + pad, :] = jnp.zeros((pad, cb), F32)
    xpad_ref[pad:pad + n, :] = xr_ref[0]
    r8 = lax.broadcasted_iota(jnp.int32, (groups, SUBLANE, cb), 1)

    def gate_inputs(c, d):
        t0 = pl.multiple_of(c * tt, tt)
        blk = xpad_ref[pl.ds(t0, tt + 2 * pad), :]
        ext = tt + 2 * pad
        xc = (bc_ref[...]
              + wc_ref[0:1, :] * pltpu.roll(blk, 2, 0)[pad:pad + tt]
              + wc_ref[1:2, :] * pltpu.roll(blk, 1, 0)[pad:pad + tt]
              + wc_ref[2:3, :] * blk[pad:pad + tt]
              + wc_ref[3:4, :] * pltpu.roll(blk, ext - 1, 0)[pad:pad + tt])
        g = jnp.dot(xc.astype(BF16), wg_ref[d, 0], preferred_element_type=F32)
        r = _sigmoid(g[:, :cb] + bg_ref[d, 0:1, :])
        i = _sigmoid(g[:, cb:] + bg_ref[d, 1:2, :])
        lam = lam_ref[d:d + 1, :]
        softplus = jnp.maximum(-lam, 0.0) + jnp.log1p(jnp.exp(-jnp.abs(lam)))
        log_a = (-LRU_C) * r * softplus
        a = jnp.exp(log_a)
        u = jnp.sqrt(-jnp.tanh(log_a) * (1.0 + a * a)) * (i * xc)
        return t0, a, u

    def scan8(a, u, reverse):
        a = a.reshape(groups, SUBLANE, cb)
        u = u.reshape(groups, SUBLANE, cb)
        for d in (1, 2, 4):
            keep = (r8 < SUBLANE - d) if reverse else (r8 >= d)
            shift = SUBLANE - d if reverse else d
            u = u + jnp.where(keep, a * pltpu.roll(u, shift, 1), 0.0)
            a = jnp.where(keep, a * pltpu.roll(a, shift, 1), a)
        return a, u

    def fwd_chunk(c, hc):
        t0, a, u = gate_inputs(c, 0)
        a, u = scan8(a, u, False)
        for g in range(groups):
            hg = a[g] * hc + u[g]
            hf_ref[pl.ds(t0 + g * SUBLANE, SUBLANE), :] = hg
            hc = hg[SUBLANE - 1:SUBLANE]
        return hc

    def bwd_chunk(j, hc):
        c = n_chunks - 1 - j
        t0, a, u = gate_inputs(c, 1)
        a, u = scan8(a, u, True)
        for g in reversed(range(groups)):
            hg = a[g] * hc + u[g]
            hb_ref[g * SUBLANE:(g + 1) * SUBLANE, :] = hg
            hc = hg[0:1]
        hsum = hf_ref[pl.ds(t0, tt), :] + hb_ref[...]
        o_ref[0, pl.ds(t0, tt), :] = (hsum * _gelu_tanh(xg_ref[0, pl.ds(t0, tt), :])).astype(o_ref.dtype)
        return hc

    h_fwd = lax.fori_loop(0, n_chunks, fwd_chunk, h0_ref[0, 0:1, :])
    h_bwd = lax.fori_loop(0, n_chunks, bwd_chunk, h0_ref[0, 1:2, :])
    hT_ref[0, 0:1, :] = h_fwd
    hT_ref[0, 1:2, :] = h_bwd


def _rglru(xr, xg, h0, w_conv, b_conv, wg_bd, b_gate, lam):
    b, n, _ = xr.shape
    tt = min(RNN_TT, n)
    n_cb = D_RNN // RNN_CB
    slab = pl.BlockSpec((1, n, RNN_CB), lambda bi, ci: (bi, 0, ci))
    state = pl.BlockSpec((1, 2, RNN_CB), lambda bi, ci: (bi, 0, ci))
    return pl.pallas_call(
        functools.partial(_rnn_kernel, n=n, tt=tt),
        grid=(b, n_cb),
        in_specs=[slab, slab, state,
                  pl.BlockSpec((CONV_W, RNN_CB), lambda bi, ci: (0, ci)),
                  pl.BlockSpec((1, RNN_CB), lambda bi, ci: (0, ci)),
                  pl.BlockSpec((2, 1, RNN_CB, 2 * RNN_CB), lambda bi, ci: (0, ci, 0, 0)),
                  pl.BlockSpec((2, 2, RNN_CB), lambda bi, ci: (0, 0, ci)),
                  pl.BlockSpec((2, RNN_CB), lambda bi, ci: (0, ci))],
        out_specs=[slab, state],
        out_shape=[jax.ShapeDtypeStruct((b, n, D_RNN), BF16), jax.ShapeDtypeStruct((b, 2, D_RNN), F32)],
        scratch_shapes=[pltpu.VMEM((n + 2 * SUBLANE, RNN_CB), F32), pltpu.VMEM((n, RNN_CB), F32),
                        pltpu.VMEM((tt, RNN_CB), F32)],
        compiler_params=_cparams(("arbitrary", "arbitrary")),
        name=f"rglru_{n}",
    )(xr, xg, h0, w_conv, b_conv.reshape(1, -1), wg_bd, b_gate, lam)


def _gate_weights(w_gate):
    per = RNN_CB // RNN_BLOCK_DIM
    n_cb = D_RNN // RNN_CB
    w = w_gate.reshape(2, 2, n_cb, per, RNN_BLOCK_DIM, RNN_BLOCK_DIM)
    eye = jnp.eye(per, dtype=w.dtype)
    bd = jnp.einsum('dkcpij,pq->dkcpiqj', w, eye).reshape(2, 2, n_cb, RNN_CB, RNN_CB)
    return jnp.concatenate([bd[:, 0], bd[:, 1]], axis=-1).astype(BF16)


def _merge_kernel(oa_ref, or_ref, ga_ref, gr_ref, x_ref, mod_ref, base_ref, gpost_ref, gpre_ref,
                  wa_ref, wr_ref, wo_ref, wrh_ref, wrl_ref, br_ref,
                  y_ref, h_ref, info_ref, gate_ref, cnt_ref, run_ref, *, row0, tiles_per_row):
    i = pl.program_id(0)
    row = row0 + i // tiles_per_row
    ya = jnp.dot(oa_ref[...], wa_ref[...], preferred_element_type=F32)
    yr = jnp.dot(or_ref[...], wr_ref[...], preferred_element_type=F32)
    y = _sigmoid(ga_ref[...]) * ya + _sigmoid(gr_ref[...]) * yr
    mix = jnp.dot(y.astype(BF16), wo_ref[...], preferred_element_type=F32)
    y1 = x_ref[...] + _mod_rows(mod_ref, row, 2) * _rms(mix, gpost_ref[...])
    y_ref[...] = y1
    h = _rms(y1, gpre_ref[...]) * (1.0 + _mod_rows(mod_ref, row, 4)) + _mod_rows(mod_ref, row, 3)
    h_ref[...] = h

    h_hi = h.astype(BF16)
    h_lo = (h - h_hi.astype(F32)).astype(BF16)
    logits = (jnp.dot(h_hi, wrh_ref[...], preferred_element_type=F32)
              + jnp.dot(h_lo, wrh_ref[...], preferred_element_type=F32)
              + jnp.dot(h_hi, wrl_ref[...], preferred_element_type=F32)) + br_ref[...]
    tm = logits.shape[0]
    lane = lax.broadcasted_iota(jnp.int32, (tm, LANE), 1)
    work = logits
    sel = []
    vals = []
    for _ in range(TOP_K):
        mx = jnp.max(work, axis=-1, keepdims=True)
        idx = jnp.min(jnp.where(work == mx, lane, LANE), axis=-1, keepdims=True)
        hit = lane == idx
        sel.append((idx, hit))
        vals.append(mx)
        work = jnp.where(hit, NEG_BIG * 2.0, work)
    exps = [jnp.exp(v - vals[0]) for v in vals]
    inv = 1.0 / (exps[0] + exps[1] + exps[2] + exps[3])

    @pl.when(i == 0)
    def _():
        run_ref[...] = base_ref[...]

    mask = jnp.zeros((tm, LANE), F32)
    for _, hit in sel:
        mask = mask + hit.astype(F32)
    rr = lax.broadcasted_iota(jnp.int32, (tm, tm), 0)
    cc = lax.broadcasted_iota(jnp.int32, (tm, tm), 1)
    tri = (cc < rr).astype(BF16)
    rank = jnp.dot(tri, mask.astype(BF16), preferred_element_type=F32) + run_ref[...]
    run_ref[...] = run_ref[...] + jnp.sum(mask, axis=0, keepdims=True)
    cnt_ref[...] = run_ref[...]

    info = jnp.zeros((tm, LANE), jnp.int32)
    gates = jnp.zeros((tm, LANE), F32)
    for kk, (idx, hit) in enumerate(sel):
        rk = jnp.sum(jnp.where(hit, rank, 0.0), axis=-1, keepdims=True).astype(jnp.int32)
        info = jnp.where(lane == kk, idx, info)
        info = jnp.where(lane == TOP_K + kk, rk, info)
        gates = jnp.where(lane == kk, exps[kk] * inv, gates)
    info_ref[...] = info
    gate_ref[...] = gates


def _merge(oa, orec, ga, gr, x, mod, base, g_post, g_pre, wa, wr, wo, wr_hi, wr_lo, b_router,
           *, row0, tokens_per_row):
    t = x.shape[0]
    tile = pl.BlockSpec((TM, D_MODEL), lambda i: (i, 0))
    small = pl.BlockSpec((TM, LANE), lambda i: (i, 0))
    vec = _full((1, D_MODEL))
    wspec = _resident((D_MODEL, D_MODEL))
    rspec = _resident((D_MODEL, LANE))
    return pl.pallas_call(
        functools.partial(_merge_kernel, row0=row0, tiles_per_row=tokens_per_row // TM),
        grid=(t // TM,),
        in_specs=[tile, tile, tile, tile, tile, _full(mod.shape), _full((1, LANE)), vec, vec,
                  wspec, wspec, wspec, rspec, rspec, _full((1, LANE))],
        out_specs=[tile, tile, small, small, _full((1, LANE))],
        out_shape=[jax.ShapeDtypeStruct((t, D_MODEL), F32), jax.ShapeDtypeStruct((t, D_MODEL), F32),
                   jax.ShapeDtypeStruct((t, LANE), jnp.int32), jax.ShapeDtypeStruct((t, LANE), F32),
                   jax.ShapeDtypeStruct((1, LANE), F32)],
        scratch_shapes=[pltpu.VMEM((1, LANE), F32)],
        compiler_params=_cparams(("arbitrary",)),
        name=f"merge_router_{t}",
    )(oa, orec, ga, gr, x, mod, base, g_post.reshape(1, -1), g_pre.reshape(1, -1),
      wa, wr, wo, wr_hi, wr_lo, b_router)


TD = 512


def _dispatch_kernel(dest_ref, ps_ref, pe_ref, nb_ref, hp_ref, hs_ref, xs_ref, zero_ref, sem, zsem,
                     *, n_p_tiles, n_blocks):
    i = pl.program_id(0)

    @pl.when(i == 0)
    def _():
        zero_ref[...] = jnp.zeros(zero_ref.shape, zero_ref.dtype)

        def zero_copy(start):
            return pltpu.make_async_copy(zero_ref, xs_ref.at[pl.ds(pl.multiple_of(start, MOE_BLOCK), MOE_BLOCK), :],
                                         zsem)

        for act in ("start", "wait"):
            for e in range(N_EXPERTS):
                @pl.when(pe_ref[e] > ps_ref[e])
                def _():
                    getattr(zero_copy(pe_ref[e] - MOE_BLOCK), act)()

                @pl.when(n_blocks - 1 - e >= nb_ref[0])
                def _():
                    getattr(zero_copy((n_blocks - 1 - e) * MOE_BLOCK), act)()

    def scatter(h_ref):
        def copy(r, k):
            return pltpu.make_async_copy(h_ref.at[pl.ds(r, 1), :],
                                         xs_ref.at[pl.ds(dest_ref[r * TOP_K + k], 1), :], sem)

        def start(r, c):
            for k in range(TOP_K):
                copy(r, k).start()
            return c

        def wait(r, c):
            for k in range(TOP_K):
                copy(r, k).wait()
            return c

        lax.fori_loop(0, TD, start, 0)
        lax.fori_loop(0, TD, wait, 0)

    @pl.when(i < n_p_tiles)
    def _():
        scatter(hp_ref)

    @pl.when(i >= n_p_tiles)
    def _():
        scatter(hs_ref)


def _dispatch(dest_flat, h_p, h_s, pad_start, pad_end, n_used, n_slots):
    n_p_tiles = h_p.shape[0] // TD
    n_tiles = n_p_tiles + h_s.shape[0] // TD
    n_blocks = n_slots // MOE_BLOCK
    assert n_blocks - N_EXPERTS >= 0
    smem = pl.BlockSpec(memory_space=pltpu.SMEM)
    return pl.pallas_call(
        functools.partial(_dispatch_kernel, n_p_tiles=n_p_tiles, n_blocks=n_blocks),
        grid=(n_tiles,),
        in_specs=[pl.BlockSpec((TD * TOP_K,), lambda i: (i,), memory_space=pltpu.SMEM), smem, smem, smem,
                  pl.BlockSpec((TD, D_MODEL), lambda i: (jnp.minimum(i, n_p_tiles - 1), 0)),
                  pl.BlockSpec((TD, D_MODEL), lambda i: (jnp.maximum(i - n_p_tiles, 0), 0))],
        out_specs=pl.BlockSpec(memory_space=pl.ANY),
        out_shape=jax.ShapeDtypeStruct((n_slots, D_MODEL), h_p.dtype),
        scratch_shapes=[pltpu.VMEM((MOE_BLOCK, D_MODEL), h_p.dtype), pltpu.SemaphoreType.DMA,
                        pltpu.SemaphoreType.DMA],
        compiler_params=_cparams(("arbitrary",)),
        name="moe_dispatch",
    )(dest_flat, pad_start, pad_end, n_used, h_p, h_s)


SEL_W = 256


def _expert_kernel(be_ref, nb_ref, par_ref, nxt_ref, xs_ref, wu_ref, bu_ref, wd_ref, bd_ref,
                   ys_ref, wuf_ref, wdf_ref, wub_ref, wdb_ref, wsem):
    i = pl.program_id(0)
    nb = nb_ref[0]
    e = be_ref[i]
    par = par_ref[i]

    def weight_copies(expert, p):
        return (pltpu.make_async_copy(wu_ref.at[expert], wuf_ref.at[p], wsem.at[0, p]),
                pltpu.make_async_copy(wd_ref.at[expert], wdf_ref.at[p], wsem.at[1, p]))

    @pl.when(i == 0)
    def _():
        for cp in weight_copies(e, par):
            cp.start()

    new_expert = (i == 0) | (e != be_ref[jnp.maximum(i - 1, 0)])

    @pl.when((i < nb) & new_expert)
    def _():
        for cp in weight_copies(e, par):
            cp.wait()

        @pl.when(nxt_ref[i] != e)
        def _():
            for cp in weight_copies(nxt_ref[i], 1 - par):
                cp.start()

        r = lax.broadcasted_iota(jnp.int32, (SEL_W, SEL_W), 0)
        c = lax.broadcasted_iota(jnp.int32, (SEL_W, SEL_W), 1)
        half = SEL_W // 2
        sel = jnp.where(r == jnp.where(c < half, 2 * c, 2 * (c - half) + 1), 1.0, 0.0).astype(BF16)
        for g in range(2 * D_FF // SEL_W):
            cols = wuf_ref[par, :, g * SEL_W:(g + 1) * SEL_W].astype(BF16)
            d = jnp.dot(cols, sel, preferred_element_type=F32).astype(BF16)
            wub_ref[:, g * half:(g + 1) * half] = d[:, :half]
            wub_ref[:, D_FF + g * half:D_FF + (g + 1) * half] = d[:, half:]
        wdb_ref[...] = wdf_ref[par].astype(BF16)

    @pl.when(i < nb)
    def _():
        hu = jnp.dot(xs_ref[...].astype(BF16), wub_ref[...], preferred_element_type=F32) + bu_ref[0]
        glu = jnp.minimum(hu[:, :D_FF], SWIGLU_LIMIT)
        lin = jnp.clip(hu[:, D_FF:], -SWIGLU_LIMIT, SWIGLU_LIMIT)
        act = (lin + 1.0) * glu * _sigmoid(SWIGLU_ALPHA * glu)
        ys_ref[...] = jnp.dot(act.astype(BF16), wdb_ref[...], preferred_element_type=F32) + bd_ref[0]

    @pl.when(i >= nb)
    def _():
        ys_ref[...] = jnp.zeros(ys_ref.shape, F32)


def _experts(blk_e, n_used, run_par, next_e, xs, wu, bu, wd, bd):
    ns = xs.shape[0]
    hbm = pl.BlockSpec(memory_space=pl.ANY)
    grid_spec = pltpu.PrefetchScalarGridSpec(
        num_scalar_prefetch=4,
        grid=(ns // MOE_BLOCK,),
        in_specs=[pl.BlockSpec((MOE_BLOCK, D_MODEL), lambda i, be, nb, pr, nx: (jnp.minimum(i, nb[0] - 1), 0)),
                  hbm,
                  pl.BlockSpec((1, 1, 2 * D_FF), lambda i, be, nb, pr, nx: (be[i], 0, 0)),
                  hbm,
                  pl.BlockSpec((1, 1, D_MODEL), lambda i, be, nb, pr, nx: (be[i], 0, 0))],
        out_specs=pl.BlockSpec((MOE_BLOCK, D_MODEL), lambda i, be, nb, pr, nx: (i, 0)),
        scratch_shapes=[pltpu.VMEM((2, D_MODEL, 2 * D_FF), F32), pltpu.VMEM((2, D_FF, D_MODEL), F32),
                        pltpu.VMEM((D_MODEL, 2 * D_FF), BF16), pltpu.VMEM((D_FF, D_MODEL), BF16),
                        pltpu.SemaphoreType.DMA((2, 2))],
    )
    return pl.pallas_call(
        _expert_kernel,
        grid_spec=grid_spec,
        out_shape=jax.ShapeDtypeStruct((ns, D_MODEL), F32),
        compiler_params=_cparams(("arbitrary",)),
        name="moe_experts",
    )(blk_e, n_used, run_par, next_e, xs, wu, bu, wd, bd)


def _combine_kernel(dest_ref, ys_ref, gate_ref, y1_ref, mod_ref, gpost_ref, o_ref, buf_ref, sem,
                    *, row0, tiles_per_row):
    row = row0 + pl.program_id(0) // tiles_per_row

    def copy(r, k):
        return pltpu.make_async_copy(ys_ref.at[pl.ds(dest_ref[r * TOP_K + k], 1), :],
                                     buf_ref.at[k, pl.ds(r, 1), :], sem)

    def start(r, c):
        for k in range(TOP_K):
            copy(r, k).start()
        return c

    def wait(r, c):
        for k in range(TOP_K):
            copy(r, k).wait()
        return c

    lax.fori_loop(0, TD, start, 0)
    lax.fori_loop(0, TD, wait, 0)
    g = gate_ref[...]
    moe = g[:, 0:1] * buf_ref[0]
    for kk in range(1, TOP_K):
        moe = moe + g[:, kk:kk + 1] * buf_ref[kk]
    o_ref[...] = y1_ref[...] + _mod_rows(mod_ref, row, 5) * _rms(moe, gpost_ref[...])


def _combine(dest_flat, ys, gates, y1, mod, g_post, *, row0, tokens_per_row):
    t = y1.shape[0]
    tile = pl.BlockSpec((TD, D_MODEL), lambda i: (i, 0))
    return pl.pallas_call(
        functools.partial(_combine_kernel, row0=row0, tiles_per_row=tokens_per_row // TD),
        grid=(t // TD,),
        in_specs=[pl.BlockSpec((TD * TOP_K,), lambda i: (i,), memory_space=pltpu.SMEM),
                  pl.BlockSpec(memory_space=pl.ANY),
                  pl.BlockSpec((TD, LANE), lambda i: (i, 0)), tile, _full(mod.shape), _full((1, D_MODEL))],
        out_specs=tile,
        out_shape=jax.ShapeDtypeStruct((t, D_MODEL), F32),
        scratch_shapes=[pltpu.VMEM((TOP_K, TD, D_MODEL), F32), pltpu.SemaphoreType.DMA],
        compiler_params=_cparams(("arbitrary",)),
        name=f"moe_combine_{t}",
    )(dest_flat, ys, gates, y1, mod, g_post.reshape(1, -1))


def kernel(x_prompt, x_sample, cache_k, cache_v, state_h, c, c_ctx, w_ada, b_ada, g_pre_mix, g_post_mix, g_pre_ffn, g_post_ffn, w_in, lam_q, lam_k, g_subln, w_conv, b_conv, w_lru_gate, b_lru_gate, lru_lambda, w_attn_proj, w_rec_proj, w_out, w_router, b_router, w_up, b_up, w_down, b_down):
    depth = w_in.shape[0]
    bp, sp, _ = x_prompt.shape
    bs, ss, _ = x_sample.shape
    tp, ts = bp * sp, bs * ss
    cos_t, sin_t = _rope_tables(ss)

    y_p = x_prompt.reshape(tp, D_MODEL)
    y_s = x_sample.reshape(ts, D_MODEL)
    ks, vs, hs = [], [], []
    for l in range(depth):
        lam_init = 0.8 - 0.6 * math.exp(-0.3 * l)
        cvec = jnp.zeros((SUBLANE, D_MODEL), F32).at[0].set(c_ctx).at[1:1 + bs].set(c)
        mod = _adaln(cvec, w_ada[l], b_ada[l])
        w_in_bf = w_in[l].astype(BF16)
        wa, wr, wo = w_attn_proj[l].astype(BF16), w_rec_proj[l].astype(BF16), w_out[l].astype(BF16)
        wg_bd = _gate_weights(w_lru_gate[l])
        w_rt = jnp.pad(w_router[l], ((0, 0), (0, LANE - N_EXPERTS)))
        wr_hi = w_rt.astype(BF16)
        wr_lo = (w_rt - wr_hi.astype(F32)).astype(BF16)
        b_rt = jnp.pad(b_router[l], (0, LANE - N_EXPERTS), constant_values=NEG_BIG).reshape(1, LANE)
        bu = b_up[l].reshape(N_EXPERTS, D_FF, 2)
        bu = jnp.concatenate([bu[..., 0], bu[..., 1]], axis=-1).reshape(N_EXPERTS, 1, 2 * D_FF)
        bd = b_down[l].reshape(N_EXPERTS, 1, D_MODEL)

        qt, k, vt, xr, xg, ga, gr, k32, v32 = _inproj(y_p, mod, g_pre_mix[l], w_in_bf, row0=0, tokens_per_row=tp,
                                                      tokens_per_batch=sp)
        oa = _attention(qt, k.reshape(bp, sp, D_MODEL), vt, lam_q[l], lam_k[l], g_subln[l],
                        lam_init=lam_init, tq=sp)
        orec, h_t = _rglru(xr.reshape(bp, sp, D_RNN), xg.reshape(bp, sp, D_RNN),
                           jnp.zeros((bp, 2, D_RNN), F32), w_conv[l], b_conv[l], wg_bd, b_lru_gate[l], lru_lambda[l])
        ks.append(k32.reshape(bp, sp, N_HEADS, 2, HEAD_DIM))
        vs.append(v32.reshape(bp, sp, N_HEADS, V_DIM))
        hs.append(h_t)
        y1_p, h2_p, info_p, gate_p, cnt_p = _merge(
            oa.reshape(tp, D_MODEL), orec.reshape(tp, D_MODEL), ga, gr, y_p, mod, jnp.zeros((1, LANE), F32),
            g_post_mix[l], g_pre_ffn[l], wa, wr, wo, wr_hi, wr_lo, b_rt, row0=0, tokens_per_row=tp)

        qt, k, vt, xr, xg, ga, gr = _inproj(y_s, mod, g_pre_mix[l], w_in_bf, row0=1, tokens_per_row=ss,
                                            tokens_per_batch=ss, rope_tabs=(cos_t, sin_t))
        oa = _attention(qt, k.reshape(bs, ss, D_MODEL), vt, lam_q[l], lam_k[l], g_subln[l],
                        lam_init=lam_init, tq=512, cache=_cache_layouts(cache_k[:, l], cache_v[:, l], TM))
        orec, _ = _rglru(xr.reshape(bs, ss, D_RNN), xg.reshape(bs, ss, D_RNN), state_h[:, l],
                         w_conv[l], b_conv[l], wg_bd, b_lru_gate[l], lru_lambda[l])
        y1_s, h2_s, info_s, gate_s, cnt_s = _merge(
            oa.reshape(ts, D_MODEL), orec.reshape(ts, D_MODEL), ga, gr, y_s, mod, cnt_p,
            g_post_mix[l], g_pre_ffn[l], wa, wr, wo, wr_hi, wr_lo, b_rt, row0=1, tokens_per_row=ss)

        counts = cnt_s[0, :N_EXPERTS].astype(jnp.int32)
        padded = (counts + MOE_BLOCK - 1) // MOE_BLOCK * MOE_BLOCK
        pad_end = jnp.cumsum(padded)
        pad_start = pad_end - padded
        n_slots = (tp + ts) * TOP_K + N_EXPERTS * MOE_BLOCK
        n_blocks = n_slots // MOE_BLOCK
        blk_start = jnp.arange(n_blocks, dtype=jnp.int32) * MOE_BLOCK
        blk_e = jnp.minimum(jnp.sum(pad_end[None, :] <= blk_start[:, None], axis=1), N_EXPERTS - 1).astype(jnp.int32)
        n_used = (pad_end[-1:] // MOE_BLOCK).astype(jnp.int32)

        def dest_of(info):
            e = info[:, :TOP_K]
            onehot = e[:, :, None] == jnp.arange(N_EXPERTS, dtype=jnp.int32)[None, None, :]
            return (jnp.sum(jnp.where(onehot, pad_start[None, None, :], 0), axis=-1)
                    + info[:, TOP_K:2 * TOP_K]).reshape(-1)

        dest_p, dest_s = dest_of(info_p), dest_of(info_s)
        run_par = (jnp.cumsum(jnp.concatenate([jnp.zeros((1,), jnp.int32),
                                               (blk_e[1:] != blk_e[:-1]).astype(jnp.int32)])) % 2).astype(jnp.int32)
        ids = jnp.arange(N_EXPERTS, dtype=jnp.int32)
        later = lax.cummin(jnp.where(padded > 0, ids, N_EXPERTS), reverse=True)
        next_active = jnp.concatenate([later[1:], jnp.full((1,), N_EXPERTS, jnp.int32)])
        next_active = jnp.where(next_active < N_EXPERTS, next_active, ids)
        next_e = jnp.sum(jnp.where(blk_e[:, None] == ids[None, :], next_active[None, :], 0), axis=1).astype(jnp.int32)
        xs = _dispatch(jnp.concatenate([dest_p, dest_s]), h2_p, h2_s, pad_start, pad_end, n_used, n_slots)
        ys = _experts(blk_e, n_used, run_par, next_e, xs, w_up[l], bu, w_down[l], bd)
        y_p = _combine(dest_p, ys, gate_p, y1_p, mod, g_post_ffn[l], row0=0, tokens_per_row=tp)
        y_s = _combine(dest_s, ys, gate_s, y1_s, mod, g_post_ffn[l], row0=1, tokens_per_row=ss)

    return (y_p.reshape(bp, sp, D_MODEL), y_s.reshape(bs, ss, D_MODEL),
            jnp.stack(ks, axis=1), jnp.stack(vs, axis=1), jnp.stack(hs, axis=1))
```

```python
import functools
import math

import jax
import jax.numpy as jnp
from jax import lax
from jax.experimental import pallas as pl
from jax.experimental.pallas import tpu as pltpu

F32 = jnp.float32
BF16 = jnp.bfloat16

D_MODEL = 1024
N_HEADS = 8
HEAD_DIM = 64
V_DIM = 2 * HEAD_DIM
GRID_W = 64
D_RNN = D_MODEL
RNN_BLOCKS = 16
RNN_BLOCK_DIM = D_RNN // RNN_BLOCKS
CONV_W = 4
LRU_C = 8.0
N_EXPERTS = 32
TOP_K = 4
D_FF = D_MODEL
SWIGLU_ALPHA = 1.702
SWIGLU_LIMIT = 7.0
ROPE_THETA = 10000.0
MOE_BLOCK = 256
EPS = 1e-6
N_MOD = 6
IN_PARTS = 7

LANE = 128
SUBLANE = 8
ROW_TILE = 8
VMEM_LIMIT = 56 * 1024 * 1024

TM = 256
RNN_CB = 256
RNN_TT = 256
NEG_BIG = -1e30
ONES_ROWS = 16
V_ROWS = V_DIM + ONES_ROWS
Q_SCALE =HEAD_DIM ** -0.5 * math.log2(math.e)


def _cparams(sem):
    return pltpu.CompilerParams(dimension_semantics=sem, vmem_limit_bytes=VMEM_LIMIT)


def _full(shape):
    return pl.BlockSpec(shape, lambda *_: (0,) * len(shape))


def _resident(shape):
    return pl.BlockSpec(shape, lambda *_: (0,) * len(shape), pipeline_mode=pl.Buffered(1))


def _rms(x, g):
    return x * lax.rsqrt(jnp.mean(x * x, axis=-1, keepdims=True) + EPS) * g


def _ada_kernel(c_ref, w_ref, b_ref, o_ref):
    c = c_ref[...]
    s = c * jax.nn.sigmoid(c)
    o_ref[...] = jnp.dot(s.astype(BF16), w_ref[...].astype(BF16), preferred_element_type=F32) + b_ref[...]


def _adaln(cvec, w_ada, b_ada):
    return pl.pallas_call(
        _ada_kernel,
        grid=(N_MOD,),
        in_specs=[_full((SUBLANE, D_MODEL)),
                  pl.BlockSpec((D_MODEL, D_MODEL), lambda j: (0, j)),
                  pl.BlockSpec((1, D_MODEL), lambda j: (0, j))],
        out_specs=pl.BlockSpec((SUBLANE, D_MODEL), lambda j: (0, j)),
        out_shape=jax.ShapeDtypeStruct((SUBLANE, N_MOD * D_MODEL), F32),
        compiler_params=_cparams(("arbitrary",)),
        name="adaln",
    )(cvec, w_ada, b_ada.reshape(1, -1))


def _mod_rows(mod_ref, row, part):
    return mod_ref[pl.ds(row, 1), part * D_MODEL:(part + 1) * D_MODEL]


def _inproj_kernel(*refs, rope, row0, tiles_per_row):
    if rope:
        (x_ref, mod_ref, g_ref, w_ref, cos_ref, sin_ref,
         q_ref, k_ref, v_ref, xr_ref, xg_ref, ga_ref, gr_ref) = refs
    else:
        (x_ref, mod_ref, g_ref, w_ref,
         q_ref, k_ref, v_ref, xr_ref, xg_ref, ga_ref, gr_ref, k32_ref, v32_ref) = refs
    row = row0 + pl.program_id(0) // tiles_per_row
    shift = _mod_rows(mod_ref, row, 0)
    scale = _mod_rows(mod_ref, row, 1)
    h = (_rms(x_ref[...], g_ref[...]) * (1.0 + scale) + shift).astype(BF16)

    def proj(j):
        return jnp.dot(h, w_ref[:, j * D_MODEL:(j + 1) * D_MODEL], preferred_element_type=F32)

    def rotate(x):
        lane = lax.broadcasted_iota(jnp.int32, (x.shape[0], LANE), 1)
        first = (lane & 31) < 16
        outs = []
        for c in range(D_MODEL // LANE):
            xc = x[:, c * LANE:(c + 1) * LANE]
            partner = jnp.where(first, pltpu.roll(xc, LANE - 16, 1), pltpu.roll(xc, 16, 1))
            outs.append(xc * cos_ref[...] + partner * sin_ref[...])
        return jnp.concatenate(outs, axis=1)

    q = proj(0)
    k = proj(1)
    v = proj(2)
    if rope:
        q = rotate(q)
        k = rotate(k)
    else:
        k32_ref[...] = k
        v32_ref[...] = v
    k_ref[...] = k.astype(BF16)
    q = q * Q_SCALE
    for hd in range(N_HEADS):
        cols = slice(hd * V_DIM, (hd + 1) * V_DIM)
        q_ref[0, hd] = q[:, cols].T.astype(BF16)
        v_ref[0, hd, 0, :V_DIM, :] = v[:, cols].T.astype(BF16)
        v_ref[0, hd, 0, V_DIM:, :] = jnp.ones((ONES_ROWS, v.shape[0]), BF16)
    xr_ref[...] = proj(3)
    xg_ref[...] = proj(4)
    ga_ref[...] = proj(5)
    gr_ref[...] = proj(6)


def _inproj(x, mod, g_pre, w_in_bf, *, row0, tokens_per_row, tokens_per_batch, rope_tabs=None):
    t = x.shape[0]
    rope = rope_tabs is not None
    tile = pl.BlockSpec((TM, D_MODEL), lambda i: (i, 0))
    in_specs = [tile, _full(mod.shape), _full((1, D_MODEL)), _resident(w_in_bf.shape)]
    args = [x, mod, g_pre.reshape(1, -1), w_in_bf]
    n_b = t // tokens_per_batch
    tpb = tokens_per_batch // TM
    outs = [jax.ShapeDtypeStruct((n_b, N_HEADS, V_DIM, tokens_per_batch), BF16),
            jax.ShapeDtypeStruct((t, D_MODEL), BF16),
            jax.ShapeDtypeStruct((n_b, N_HEADS, tpb, V_ROWS, TM), BF16)] + [jax.ShapeDtypeStruct((t, D_MODEL), F32)] * 4
    out_specs = [pl.BlockSpec((1, N_HEADS, V_DIM, TM), lambda i: (i // tpb, 0, 0, i % tpb)), tile,
                 pl.BlockSpec((1, N_HEADS, 1, V_ROWS, TM), lambda i: (i // tpb, 0, i % tpb, 0, 0))] + [tile] * 4
    if rope:
        n_pos = rope_tabs[0].shape[0] // TM
        tab = pl.BlockSpec((TM, LANE), lambda i: (i % n_pos, 0))
        in_specs += [tab, tab]
        args += list(rope_tabs)
    else:
        outs += [jax.ShapeDtypeStruct((t, D_MODEL), F32)] * 2
        out_specs += [tile] * 2
    return pl.pallas_call(
        functools.partial(_inproj_kernel, rope=rope, row0=row0, tiles_per_row=tokens_per_row // TM),
        grid=(t // TM,),
        in_specs=in_specs,
        out_specs=out_specs,
        out_shape=outs,
        compiler_params=_cparams(("arbitrary",)),
        name="inproj_lat" if rope else "inproj_ctx",
    )(*args)


def _rope_tables(n_tokens):
    rows = n_tokens // GRID_W
    row = jnp.repeat(jnp.arange(rows, dtype=F32), GRID_W)
    col = jnp.tile(jnp.arange(GRID_W, dtype=F32), rows)
    q4 = HEAD_DIM // 4
    inv_freq = jnp.power(ROPE_THETA, -jnp.arange(q4, dtype=F32) / q4)
    ang = jnp.concatenate([row[:, None] * inv_freq, col[:, None] * inv_freq], axis=-1)
    cos, sin = jnp.cos(ang), jnp.sin(ang)
    c64 = jnp.concatenate([cos[:, :q4], cos[:, :q4], cos[:, q4:], cos[:, q4:]], axis=-1)
    s64 = jnp.concatenate([-sin[:, :q4], sin[:, :q4], -sin[:, q4:], sin[:, q4:]], axis=-1)
    return jnp.tile(c64, (1, LANE // HEAD_DIM)), jnp.tile(s64, (1, LANE // HEAD_DIM))


def _attn_kernel(*refs, tk, n_new, n_cache, lam_init):
    if n_cache:
        lq_ref, lk_ref, gs_ref, q_ref, k_ref, v_ref, kc_ref, vc_ref, o_ref, qs_ref, m_ref, acc_ref = refs
    else:
        lq_ref, lk_ref, gs_ref, q_ref, k_ref, v_ref, o_ref, qs_ref, m_ref, acc_ref = refs
    n_chunks = n_new + n_cache
    qt = q_ref[0, 0]
    tq = qt.shape[1]
    sub = lax.broadcasted_iota(jnp.int32, qt.shape, 0)
    zero = jnp.zeros_like(qt)
    qs_ref[:, :tq] = jnp.where(sub < HEAD_DIM, qt, zero)
    qs_ref[:, tq:] = jnp.where(sub >= HEAD_DIM, qt, zero)
    m_ref[...] = jnp.full(m_ref.shape, -jnp.inf, F32)
    acc_ref[...] = jnp.zeros(acc_ref.shape, F32)

    def scores(c):
        if c < n_new:
            kc = k_ref[0, c * tk:(c + 1) * tk, :]
        else:
            kc = kc_ref[0, (c - n_new) * tk:(c - n_new + 1) * tk, :]
        return jnp.dot(kc, qs_ref[...], preferred_element_type=F32)

    def values(c):
        return v_ref[0, 0, c] if c < n_new else vc_ref[0, 0, c - n_new]

    def accumulate(c, p, alpha):
        acc_ref[...] = alpha * acc_ref[...] + jnp.dot(values(c), p, preferred_element_type=F32)

    st = scores(0)
    pending = None
    for c in range(n_chunks):
        st_next = scores(c + 1) if c + 1 < n_chunks else None
        if pending is not None:
            accumulate(*pending)
        m_old = m_ref[...]
        m_new = jnp.maximum(m_old, jnp.max(st, axis=0, keepdims=True))
        pending = (c, jnp.exp2(st - m_new).astype(BF16), jnp.exp2(m_old - m_new))
        m_ref[...] = m_new
        st = st_next
    accumulate(*pending)

    e = jnp.exp(jnp.sum(lq_ref[...] * lk_ref[...], axis=-1, keepdims=True))
    lam = e[0:1, :] - e[1:2, :] + lam_init
    acc = acc_ref[...]
    den = acc[V_DIM:V_DIM + 1, :]
    ot = acc[:V_DIM, :tq] / den[:, :tq] - lam * (acc[:V_DIM, tq:] / den[:, tq:])
    o_ref[0] = (_rms(ot.T, gs_ref[...]) * (1.0 - lam_init)).astype(o_ref.dtype)


def _cache_layouts(cache_k, cache_v, tk):
    b, p = cache_k.shape[:2]
    kc = cache_k.reshape(b, p, D_MODEL).astype(BF16)
    vt = cache_v.transpose(0, 2, 3, 1)
    vt = jnp.concatenate([vt, jnp.ones((b, N_HEADS, ONES_ROWS, p), vt.dtype)], axis=2)
    return kc, vt.reshape(b, N_HEADS, V_ROWS, p // tk, tk).transpose(0, 1, 3, 2, 4).astype(BF16)


def _attention(qt, k, vt, lam_q, lam_k, g_sub, *, lam_init, tq, cache=None):
    b, _, _, nq = qt.shape
    n = k.shape[1]
    tk = vt.shape[-1]
    n_new = n // tk
    in_specs = [_full((2, HEAD_DIM)), _full((2, HEAD_DIM)), _full((1, V_DIM)),
                pl.BlockSpec((1, 1, V_DIM, tq), lambda bi, h, qi: (bi, h, 0, qi)),
                pl.BlockSpec((1, n, LANE), lambda bi, h, qi: (bi, 0, h)),
                pl.BlockSpec((1, 1, n_new, V_ROWS, tk), lambda bi, h, qi: (bi, h, 0, 0, 0))]
    args = [lam_q, lam_k, g_sub.reshape(1, -1), qt, k, vt]
    n_cache = 0
    if cache is not None:
        kc, vc = cache
        n_cache = vc.shape[2]
        in_specs += [pl.BlockSpec((1, kc.shape[1], LANE), lambda bi, h, qi: (bi, 0, h)),
                     pl.BlockSpec((1, 1, n_cache, V_ROWS, tk), lambda bi, h, qi: (bi, h, 0, 0, 0))]
        args += [kc, vc]
    kern = functools.partial(_attn_kernel, tk=tk, n_new=n_new, n_cache=n_cache, lam_init=lam_init)
    return pl.pallas_call(
        kern,
        grid=(b, N_HEADS, nq // tq),
        in_specs=in_specs,
        out_specs=pl.BlockSpec((1, tq, LANE), lambda bi, h, qi: (bi, qi, h)),
        out_shape=jax.ShapeDtypeStruct((b, nq, D_MODEL), BF16),
        scratch_shapes=[pltpu.VMEM((V_DIM, 2 * tq), BF16), pltpu.VMEM((1, 2 * tq), F32),
                        pltpu.VMEM((V_ROWS, 2 * tq), F32)],
        compiler_params=_cparams(("arbitrary", "arbitrary", "arbitrary")),
        name=f"diff_attn_{n + n_cache * tk}",
    )(*args)


def _sigmoid(x):
    return 0.5 * jnp.tanh(0.5 * x) + 0.5


def _gelu_tanh(x):
    return 0.5 * x * (1.0 + jnp.tanh(math.sqrt(2.0 / math.pi) * (x + 0.044715 * (x * x * x))))


def _rnn_kernel(xr_ref, xg_ref, h0_ref, wc_ref, bc_ref, wg_ref, bg_ref, lam_ref,
                o_ref, hT_ref, xpad_ref, hf_ref, hb_ref, *, n, tt):
    pad = SUBLANE
    cb = xr_ref.shape[2]
    n_chunks = n // tt
    groups = tt // SUBLANE
    xpad_ref[0:pad, :] = jnp.zeros((pad, cb), F32)
    xpad_ref[pad + n:pad + n + pad, :] = jnp.zeros((pad, cb), F32)
    xpad_ref[pad:pad + n, :] = xr_ref[0]
    r8 = lax.broadcasted_iota(jnp.int32, (groups, SUBLANE, cb), 1)

    def gate_inputs(c, d):
        t0 = pl.multiple_of(c * tt, tt)
        blk = xpad_ref[pl.ds(t0, tt + 2 * pad), :]
        ext = tt + 2 * pad
        xc = (bc_ref[...]
              + wc_ref[0:1, :] * pltpu.roll(blk, 2, 0)[pad:pad + tt]
              + wc_ref[1:2, :] * pltpu.roll(blk, 1, 0)[pad:pad + tt]
              + wc_ref[2:3, :] * blk[pad:pad + tt]
              + wc_ref[3:4, :] * pltpu.roll(blk, ext - 1, 0)[pad:pad + tt])
        g = jnp.dot(xc.astype(BF16), wg_ref[d, 0], preferred_element_type=F32)
        r = _sigmoid(g[:, :cb] + bg_ref[d, 0:1, :])
        i = _sigmoid(g[:, cb:] + bg_ref[d, 1:2, :])
        lam = lam_ref[d:d + 1, :]
        softplus = jnp.maximum(-lam, 0.0) + jnp.log1p(jnp.exp(-jnp.abs(lam)))
        log_a = (-LRU_C) * r * softplus
        a = jnp.exp(log_a)
        u = jnp.sqrt(-jnp.tanh(log_a) * (1.0 + a * a)) * (i * xc)
        return t0, a, u

    def scan8(a, u, reverse):
        a = a.reshape(groups, SUBLANE, cb)
        u = u.reshape(groups, SUBLANE, cb)
        for d in (1, 2, 4):
            keep = (r8 < SUBLANE - d) if reverse else (r8 >= d)
            shift = SUBLANE - d if reverse else d
            u = u + jnp.where(keep, a * pltpu.roll(u, shift, 1), 0.0)
            a = jnp.where(keep, a * pltpu.roll(a, shift, 1), a)
        return a, u

    def fwd_chunk(c, hc):
        t0, a, u = gate_inputs(c, 0)
        a, u = scan8(a, u, False)
        for g in range(groups):
            hg = a[g] * hc + u[g]
            hf_ref[pl.ds(t0 + g * SUBLANE, SUBLANE), :] = hg
            hc = hg[SUBLANE - 1:SUBLANE]
        return hc

    def bwd_chunk(j, hc):
        c = n_chunks - 1 - j
        t0, a, u = gate_inputs(c, 1)
        a, u = scan8(a, u, True)
        for g in reversed(range(groups)):
            hg = a[g] * hc + u[g]
            hb_ref[g * SUBLANE:(g + 1) * SUBLANE, :] = hg
            hc = hg[0:1]
        hsum = hf_ref[pl.ds(t0, tt), :] + hb_ref[...]
        o_ref[0, pl.ds(t0, tt), :] = (hsum * _gelu_tanh(xg_ref[0, pl.ds(t0, tt), :])).astype(o_ref.dtype)
        return hc

    h_fwd = lax.fori_loop(0, n_chunks, fwd_chunk, h0_ref[0, 0:1, :])
    h_bwd = lax.fori_loop(0, n_chunks, bwd_chunk, h0_ref[0, 1:2, :])
    hT_ref[0, 0:1, :] = h_fwd
    hT_ref[0, 1:2, :] = h_bwd


def _rglru(xr, xg, h0, w_conv, b_conv, wg_bd, b_gate, lam):
    b, n, _ = xr.shape
    tt = min(RNN_TT, n)
    n_cb = D_RNN // RNN_CB
    slab = pl.BlockSpec((1, n, RNN_CB), lambda bi, ci: (bi, 0, ci))
    state = pl.BlockSpec((1, 2, RNN_CB), lambda bi, ci: (bi, 0, ci))
    return pl.pallas_call(
        functools.partial(_rnn_kernel, n=n, tt=tt),
        grid=(b, n_cb),
        in_specs=[slab, slab, state,
                  pl.BlockSpec((CONV_W, RNN_CB), lambda bi, ci: (0, ci)),
                  pl.BlockSpec((1, RNN_CB), lambda bi, ci: (0, ci)),
                  pl.BlockSpec((2, 1, RNN_CB, 2 * RNN_CB), lambda bi, ci: (0, ci, 0, 0)),
                  pl.BlockSpec((2, 2, RNN_CB), lambda bi, ci: (0, 0, ci)),
                  pl.BlockSpec((2, RNN_CB), lambda bi, ci: (0, ci))],
        out_specs=[slab, state],
        out_shape=[jax.ShapeDtypeStruct((b, n, D_RNN), BF16), jax.ShapeDtypeStruct((b, 2, D_RNN), F32)],
        scratch_shapes=[pltpu.VMEM((n + 2 * SUBLANE, RNN_CB), F32), pltpu.VMEM((n, RNN_CB), F32),
                        pltpu.VMEM((tt, RNN_CB), F32)],
        compiler_params=_cparams(("arbitrary", "arbitrary")),
        name=f"rglru_{n}",
    )(xr, xg, h0, w_conv, b_conv.reshape(1, -1), wg_bd, b_gate, lam)


def _gate_weights(w_gate):
    per = RNN_CB // RNN_BLOCK_DIM
    n_cb = D_RNN // RNN_CB
    w = w_gate.reshape(2, 2, n_cb, per, RNN_BLOCK_DIM, RNN_BLOCK_DIM)
    eye = jnp.eye(per, dtype=w.dtype)
    bd = jnp.einsum('dkcpij,pq->dkcpiqj', w, eye).reshape(2, 2, n_cb, RNN_CB, RNN_CB)
    return jnp.concatenate([bd[:, 0], bd[:, 1]], axis=-1).astype(BF16)


def _merge_kernel(oa_ref, or_ref, ga_ref, gr_ref, x_ref, mod_ref, base_ref, gpost_ref, gpre_ref,
                  wa_ref, wr_ref, wo_ref, wrh_ref, wrl_ref, br_ref,
                  y_ref, h_ref, info_ref, gate_ref, cnt_ref, run_ref, *, row0, tiles_per_row):
    i = pl.program_id(0)
    row = row0 + i // tiles_per_row
    ya = jnp.dot(oa_ref[...], wa_ref[...], preferred_element_type=F32)
    yr = jnp.dot(or_ref[...], wr_ref[...], preferred_element_type=F32)
    y = _sigmoid(ga_ref[...]) * ya + _sigmoid(gr_ref[...]) * yr
    mix = jnp.dot(y.astype(BF16), wo_ref[...], preferred_element_type=F32)
    y1 = x_ref[...] + _mod_rows(mod_ref, row, 2) * _rms(mix, gpost_ref[...])
    y_ref[...] = y1
    h = _rms(y1, gpre_ref[...]) * (1.0 + _mod_rows(mod_ref, row, 4)) + _mod_rows(mod_ref, row, 3)
    for j in range(ROW_TILE):
        h_ref[:, j, :] = h[:, j * LANE:(j + 1) * LANE]

    h_hi = h.astype(BF16)
    h_lo = (h - h_hi.astype(F32)).astype(BF16)
    logits = (jnp.dot(h_hi, wrh_ref[...], preferred_element_type=F32)
              + jnp.dot(h_lo, wrh_ref[...], preferred_element_type=F32)
              + jnp.dot(h_hi, wrl_ref[...], preferred_element_type=F32)) + br_ref[...]
    tm = logits.shape[0]
    lane = lax.broadcasted_iota(jnp.int32, (tm, LANE), 1)
    work = logits
    sel = []
    vals = []
    for _ in range(TOP_K):
        mx = jnp.max(work, axis=-1, keepdims=True)
        idx = jnp.min(jnp.where(work == mx, lane, LANE), axis=-1, keepdims=True)
        hit = lane == idx
        sel.append((idx, hit))
        vals.append(mx)
        work = jnp.where(hit, NEG_BIG * 2.0, work)
    exps = [jnp.exp(v - vals[0]) for v in vals]
    inv = 1.0 / (exps[0] + exps[1] + exps[2] + exps[3])

    @pl.when(i == 0)
    def _():
        run_ref[...] = base_ref[...]

    mask = jnp.zeros((tm, LANE), F32)
    for _, hit in sel:
        mask = mask + hit.astype(F32)
    rr = lax.broadcasted_iota(jnp.int32, (tm, tm), 0)
    cc = lax.broadcasted_iota(jnp.int32, (tm, tm), 1)
    tri = (cc < rr).astype(BF16)
    rank = jnp.dot(tri, mask.astype(BF16), preferred_element_type=F32) + run_ref[...]
    run_ref[...] = run_ref[...] + jnp.sum(mask, axis=0, keepdims=True)
    cnt_ref[...] = run_ref[...]

    info = jnp.zeros((tm, LANE), jnp.int32)
    gates = jnp.zeros((tm, LANE), F32)
    for kk, (idx, hit) in enumerate(sel):
        rk = jnp.sum(jnp.where(hit, rank, 0.0), axis=-1, keepdims=True).astype(jnp.int32)
        info = jnp.where(lane == kk, idx, info)
        info = jnp.where(lane == TOP_K + kk, rk, info)
        gates = jnp.where(lane == kk, exps[kk] * inv, gates)
    info_ref[...] = info
    gate_ref[...] = gates


def _merge(oa, orec, ga, gr, x, mod, base, g_post, g_pre, wa, wr, wo, wr_hi, wr_lo, b_router,
           *, row0, tokens_per_row):
    t = x.shape[0]
    tile = pl.BlockSpec((TM, D_MODEL), lambda i: (i, 0))
    small = pl.BlockSpec((TM, LANE), lambda i: (i, 0))
    vec = _full((1, D_MODEL))
    wspec = _resident((D_MODEL, D_MODEL))
    rspec = _resident((D_MODEL, LANE))
    return pl.pallas_call(
        functools.partial(_merge_kernel, row0=row0, tiles_per_row=tokens_per_row // TM),
        grid=(t // TM,),
        in_specs=[tile, tile, tile, tile, tile, _full(mod.shape), _full((1, LANE)), vec, vec,
                  wspec, wspec, wspec, rspec, rspec, _full((1, LANE))],
        out_specs=[tile, pl.BlockSpec((TM, ROW_TILE, LANE), lambda i: (i, 0, 0)), small, small, _full((1, LANE))],
        out_shape=[jax.ShapeDtypeStruct((t, D_MODEL), F32), jax.ShapeDtypeStruct((t, ROW_TILE, LANE), F32),
                   jax.ShapeDtypeStruct((t, LANE), jnp.int32), jax.ShapeDtypeStruct((t, LANE), F32),
                   jax.ShapeDtypeStruct((1, LANE), F32)],
        scratch_shapes=[pltpu.VMEM((1, LANE), F32)],
        compiler_params=_cparams(("arbitrary",)),
        name=f"merge_router_{t}",
    )(oa, orec, ga, gr, x, mod, base, g_post.reshape(1, -1), g_pre.reshape(1, -1),
      wa, wr, wo, wr_hi, wr_lo, b_router)


TD = 512


def _dispatch_kernel(dest_ref, ps_ref, pe_ref, nb_ref, hp_ref, hs_ref, xs_ref, zero_ref, sem, zsem,
                     *, n_p_tiles, n_blocks):
    i = pl.program_id(0)

    @pl.when(i == 0)
    def _():
        zero_ref[...] = jnp.zeros(zero_ref.shape, zero_ref.dtype)

        def zero_copy(start):
            return pltpu.make_async_copy(zero_ref, xs_ref.at[pl.ds(pl.multiple_of(start, MOE_BLOCK), MOE_BLOCK)], zsem)

        for act in ("start", "wait"):
            for e in range(N_EXPERTS):
                @pl.when(pe_ref[e] > ps_ref[e])
                def _():
                    getattr(zero_copy(pe_ref[e] - MOE_BLOCK), act)()

                @pl.when(n_blocks - 1 - e >= nb_ref[0])
                def _():
                    getattr(zero_copy((n_blocks - 1 - e) * MOE_BLOCK), act)()

    def scatter(h_ref):
        def copy(r, k):
            return pltpu.make_async_copy(h_ref.at[r], xs_ref.at[dest_ref[r * TOP_K + k]], sem)

        def start(r, c):
            for k in range(TOP_K):
                copy(r, k).start()
            return c

        def wait(r, c):
            for k in range(TOP_K):
                copy(r, k).wait()
            return c

        lax.fori_loop(0, TD, start, 0)
        lax.fori_loop(0, TD, wait, 0)

    @pl.when(i < n_p_tiles)
    def _():
        scatter(hp_ref)

    @pl.when(i >= n_p_tiles)
    def _():
        scatter(hs_ref)


def _dispatch(dest_flat, h_p, h_s, pad_start, pad_end, n_used, n_slots):
    n_p_tiles = h_p.shape[0] // TD
    n_tiles = n_p_tiles + h_s.shape[0] // TD
    n_blocks = n_slots // MOE_BLOCK
    assert n_blocks - N_EXPERTS >= 0
    smem = pl.BlockSpec(memory_space=pltpu.SMEM)
    return pl.pallas_call(
        functools.partial(_dispatch_kernel, n_p_tiles=n_p_tiles, n_blocks=n_blocks),
        grid=(n_tiles,),
        in_specs=[pl.BlockSpec((TD * TOP_K,), lambda i: (i,), memory_space=pltpu.SMEM), smem, smem, smem,
                  pl.BlockSpec((TD, ROW_TILE, LANE), lambda i: (jnp.minimum(i, n_p_tiles - 1), 0, 0)),
                  pl.BlockSpec((TD, ROW_TILE, LANE), lambda i: (jnp.maximum(i - n_p_tiles, 0), 0, 0))],
        out_specs=pl.BlockSpec(memory_space=pl.ANY),
        out_shape=jax.ShapeDtypeStruct((n_slots, ROW_TILE, LANE), h_p.dtype),
        scratch_shapes=[pltpu.VMEM((MOE_BLOCK, ROW_TILE, LANE), h_p.dtype), pltpu.SemaphoreType.DMA,
                        pltpu.SemaphoreType.DMA],
        compiler_params=_cparams(("arbitrary",)),
        name="moe_dispatch",
    )(dest_flat, pad_start, pad_end, n_used, h_p, h_s)


SEL_W = 256


def _expert_kernel(be_ref, nb_ref, par_ref, nxt_ref, xs_ref, wu_ref, bu_ref, wd_ref, bd_ref,
                   ys_ref, wuf_ref, wdf_ref, wub_ref, wdb_ref, wsem):
    i = pl.program_id(0)
    nb = nb_ref[0]
    e = be_ref[i]
    par = par_ref[i]

    def weight_copies(expert, p):
        return (pltpu.make_async_copy(wu_ref.at[expert], wuf_ref.at[p], wsem.at[0, p]),
                pltpu.make_async_copy(wd_ref.at[expert], wdf_ref.at[p], wsem.at[1, p]))

    @pl.when(i == 0)
    def _():
        for cp in weight_copies(e, par):
            cp.start()

    new_expert = (i == 0) | (e != be_ref[jnp.maximum(i - 1, 0)])

    @pl.when((i < nb) & new_expert)
    def _():
        for cp in weight_copies(e, par):
            cp.wait()

        @pl.when(nxt_ref[i] != e)
        def _():
            for cp in weight_copies(nxt_ref[i], 1 - par):
                cp.start()

        r = lax.broadcasted_iota(jnp.int32, (SEL_W, SEL_W), 0)
        c = lax.broadcasted_iota(jnp.int32, (SEL_W, SEL_W), 1)
        half = SEL_W // 2
        sel = jnp.where(r == jnp.where(c < half, 2 * c, 2 * (c - half) + 1), 1.0, 0.0).astype(BF16)
        for g in range(2 * D_FF // SEL_W):
            cols = wuf_ref[par, :, g * SEL_W:(g + 1) * SEL_W].astype(BF16)
            d = jnp.dot(cols, sel, preferred_element_type=F32).astype(BF16)
            wub_ref[:, g * half:(g + 1) * half] = d[:, :half]
            wub_ref[:, D_FF + g * half:D_FF + (g + 1) * half] = d[:, half:]
        wdb_ref[...] = wdf_ref[par].astype(BF16)

    @pl.when(i < nb)
    def _():
        x = jnp.concatenate([xs_ref[:, j, :] for j in range(ROW_TILE)], axis=1)
        hu = jnp.dot(x.astype(BF16), wub_ref[...], preferred_element_type=F32) + bu_ref[0]
        glu = jnp.minimum(hu[:, :D_FF], SWIGLU_LIMIT)
        lin = jnp.clip(hu[:, D_FF:], -SWIGLU_LIMIT, SWIGLU_LIMIT)
        act = (lin + 1.0) * glu * _sigmoid(SWIGLU_ALPHA * glu)
        ys_ref[...] = jnp.dot(act.astype(BF16), wdb_ref[...], preferred_element_type=F32) + bd_ref[0]

    @pl.when(i >= nb)
    def _():
        ys_ref[...] = jnp.zeros(ys_ref.shape, F32)


def _experts(blk_e, n_used, run_par, next_e, xs, wu, bu, wd, bd):
    ns = xs.shape[0]
    hbm = pl.BlockSpec(memory_space=pl.ANY)
    grid_spec = pltpu.PrefetchScalarGridSpec(
        num_scalar_prefetch=4,
        grid=(ns // MOE_BLOCK,),
        in_specs=[pl.BlockSpec((MOE_BLOCK, ROW_TILE, LANE), lambda i, be, nb, pr, nx: (jnp.minimum(i, nb[0] - 1), 0, 0)),
                  hbm,
                  pl.BlockSpec((1, 1, 2 * D_FF), lambda i, be, nb, pr, nx: (be[i], 0, 0)),
                  hbm,
                  pl.BlockSpec((1, 1, D_MODEL), lambda i, be, nb, pr, nx: (be[i], 0, 0))],
        out_specs=pl.BlockSpec((MOE_BLOCK, D_MODEL), lambda i, be, nb, pr, nx: (i, 0)),
        scratch_shapes=[pltpu.VMEM((2, D_MODEL, 2 * D_FF), F32), pltpu.VMEM((2, D_FF, D_MODEL), F32),
                        pltpu.VMEM((D_MODEL, 2 * D_FF), BF16), pltpu.VMEM((D_FF, D_MODEL), BF16),
                        pltpu.SemaphoreType.DMA((2, 2))],
    )
    return pl.pallas_call(
        _expert_kernel,
        grid_spec=grid_spec,
        out_shape=jax.ShapeDtypeStruct((ns, D_MODEL), F32),
        compiler_params=_cparams(("arbitrary",)),
        name="moe_experts",
    )(blk_e, n_used, run_par, next_e, xs, wu, bu, wd, bd)


def _combine_kernel(dest_ref, ys_ref, gate_ref, y1_ref, mod_ref, gpost_ref, o_ref, buf_ref, sem,
                    *, row0, tiles_per_row):
    row = row0 + pl.program_id(0) // tiles_per_row

    def copy(r, k):
        return pltpu.make_async_copy(ys_ref.at[pl.ds(dest_ref[r * TOP_K + k], 1), :],
                                     buf_ref.at[k, pl.ds(r, 1), :], sem)

    def start(r, c):
        for k in range(TOP_K):
            copy(r, k).start()
        return c

    def wait(r, c):
        for k in range(TOP_K):
            copy(r, k).wait()
        return c

    lax.fori_loop(0, TD, start, 0)
    lax.fori_loop(0, TD, wait, 0)
    g = gate_ref[...]
    moe = g[:, 0:1] * buf_ref[0]
    for kk in range(1, TOP_K):
        moe = moe + g[:, kk:kk + 1] * buf_ref[kk]
    o_ref[...] = y1_ref[...] + _mod_rows(mod_ref, row, 5) * _rms(moe, gpost_ref[...])


def _combine(dest_flat, ys, gates, y1, mod, g_post, *, row0, tokens_per_row):
    t = y1.shape[0]
    tile = pl.BlockSpec((TD, D_MODEL), lambda i: (i, 0))
    return pl.pallas_call(
        functools.partial(_combine_kernel, row0=row0, tiles_per_row=tokens_per_row // TD),
        grid=(t // TD,),
        in_specs=[pl.BlockSpec((TD * TOP_K,), lambda i: (i,), memory_space=pltpu.SMEM),
                  pl.BlockSpec(memory_space=pl.ANY),
                  pl.BlockSpec((TD, LANE), lambda i: (i, 0)), tile, _full(mod.shape), _full((1, D_MODEL))],
        out_specs=tile,
        out_shape=jax.ShapeDtypeStruct((t, D_MODEL), F32),
        scratch_shapes=[pltpu.VMEM((TOP_K, TD, D_MODEL), F32), pltpu.SemaphoreType.DMA],
        compiler_params=_cparams(("arbitrary",)),
        name=f"moe_combine_{t}",
    )(dest_flat, ys, gates, y1, mod, g_post.reshape(1, -1))


def kernel(x_prompt, x_sample, cache_k, cache_v, state_h, c, c_ctx, w_ada, b_ada, g_pre_mix, g_post_mix, g_pre_ffn, g_post_ffn, w_in, lam_q, lam_k, g_subln, w_conv, b_conv, w_lru_gate, b_lru_gate, lru_lambda, w_attn_proj, w_rec_proj, w_out, w_router, b_router, w_up, b_up, w_down, b_down):
    depth = w_in.shape[0]
    bp, sp, _ = x_prompt.shape
    bs, ss, _ = x_sample.shape
    tp, ts = bp * sp, bs * ss
    cos_t, sin_t = _rope_tables(ss)

    y_p = x_prompt.reshape(tp, D_MODEL)
    y_s = x_sample.reshape(ts, D_MODEL)
    ks, vs, hs = [], [], []
    for l in range(depth):
        lam_init = 0.8 - 0.6 * math.exp(-0.3 * l)
        cvec = jnp.zeros((SUBLANE, D_MODEL), F32).at[0].set(c_ctx).at[1:1 + bs].set(c)
        mod = _adaln(cvec, w_ada[l], b_ada[l])
        w_in_bf = w_in[l].astype(BF16)
        wa, wr, wo = w_attn_proj[l].astype(BF16), w_rec_proj[l].astype(BF16), w_out[l].astype(BF16)
        wg_bd = _gate_weights(w_lru_gate[l])
        w_rt = jnp.pad(w_router[l], ((0, 0), (0, LANE - N_EXPERTS)))
        wr_hi = w_rt.astype(BF16)
        wr_lo = (w_rt - wr_hi.astype(F32)).astype(BF16)
        b_rt = jnp.pad(b_router[l], (0, LANE - N_EXPERTS), constant_values=NEG_BIG).reshape(1, LANE)
        bu = b_up[l].reshape(N_EXPERTS, D_FF, 2)
        bu = jnp.concatenate([bu[..., 0], bu[..., 1]], axis=-1).reshape(N_EXPERTS, 1, 2 * D_FF)
        bd = b_down[l].reshape(N_EXPERTS, 1, D_MODEL)

        qt, k, vt, xr, xg, ga, gr, k32, v32 = _inproj(y_p, mod, g_pre_mix[l], w_in_bf, row0=0, tokens_per_row=tp,
                                                      tokens_per_batch=sp)
        oa = _attention(qt, k.reshape(bp, sp, D_MODEL), vt, lam_q[l], lam_k[l], g_subln[l],
                        lam_init=lam_init, tq=sp)
        orec, h_t = _rglru(xr.reshape(bp, sp, D_RNN), xg.reshape(bp, sp, D_RNN),
                           jnp.zeros((bp, 2, D_RNN), F32), w_conv[l], b_conv[l], wg_bd, b_lru_gate[l], lru_lambda[l])
        ks.append(k32.reshape(bp, sp, N_HEADS, 2, HEAD_DIM))
        vs.append(v32.reshape(bp, sp, N_HEADS, V_DIM))
        hs.append(h_t)
        y1_p, h2_p, info_p, gate_p, cnt_p = _merge(
            oa.reshape(tp, D_MODEL), orec.reshape(tp, D_MODEL), ga, gr, y_p, mod, jnp.zeros((1, LANE), F32),
            g_post_mix[l], g_pre_ffn[l], wa, wr, wo, wr_hi, wr_lo, b_rt, row0=0, tokens_per_row=tp)

        qt, k, vt, xr, xg, ga, gr = _inproj(y_s, mod, g_pre_mix[l], w_in_bf, row0=1, tokens_per_row=ss,
                                            tokens_per_batch=ss, rope_tabs=(cos_t, sin_t))
        oa = _attention(qt, k.reshape(bs, ss, D_MODEL), vt, lam_q[l], lam_k[l], g_subln[l],
                        lam_init=lam_init, tq=1024, cache=_cache_layouts(cache_k[:, l], cache_v[:, l], TM))
        orec, _ = _rglru(xr.reshape(bs, ss, D_RNN), xg.reshape(bs, ss, D_RNN), state_h[:, l],
                         w_conv[l], b_conv[l], wg_bd, b_lru_gate[l], lru_lambda[l])
        y1_s, h2_s, info_s, gate_s, cnt_s = _merge(
            oa.reshape(ts, D_MODEL), orec.reshape(ts, D_MODEL), ga, gr, y_s, mod, cnt_p,
            g_post_mix[l], g_pre_ffn[l], wa, wr, wo, wr_hi, wr_lo, b_rt, row0=1, tokens_per_row=ss)

        counts = cnt_s[0, :N_EXPERTS].astype(jnp.int32)
        padded = (counts + MOE_BLOCK - 1) // MOE_BLOCK * MOE_BLOCK
        pad_end = jnp.cumsum(padded)
        pad_start = pad_end - padded
        n_slots = (tp + ts) * TOP_K + N_EXPERTS * MOE_BLOCK
        n_blocks = n_slots // MOE_BLOCK
        blk_start = jnp.arange(n_blocks, dtype=jnp.int32) * MOE_BLOCK
        blk_e = jnp.minimum(jnp.sum(pad_end[None, :] <= blk_start[:, None], axis=1), N_EXPERTS - 1).astype(jnp.int32)
        n_used = (pad_end[-1:] // MOE_BLOCK).astype(jnp.int32)

        def dest_of(info):
            e = info[:, :TOP_K]
            onehot = e[:, :, None] == jnp.arange(N_EXPERTS, dtype=jnp.int32)[None, None, :]
            return (jnp.sum(jnp.where(onehot, pad_start[None, None, :], 0), axis=-1)
                    + info[:, TOP_K:2 * TOP_K]).reshape(-1)

        dest_p, dest_s = dest_of(info_p), dest_of(info_s)
        run_par = (jnp.cumsum(jnp.concatenate([jnp.zeros((1,), jnp.int32),
                                               (blk_e[1:] != blk_e[:-1]).astype(jnp.int32)])) % 2).astype(jnp.int32)
        ids = jnp.arange(N_EXPERTS, dtype=jnp.int32)
        later = lax.cummin(jnp.where(padded > 0, ids, N_EXPERTS), reverse=True)
        next_active = jnp.concatenate([later[1:], jnp.full((1,), N_EXPERTS, jnp.int32)])
        next_active = jnp.where(next_active < N_EXPERTS, next_active, ids)
        next_e = jnp.sum(jnp.where(blk_e[:, None] == ids[None, :], next_active[None, :], 0), axis=1).astype(jnp.int32)
        xs = _dispatch(jnp.concatenate([dest_p, dest_s]), h2_p, h2_s, pad_start, pad_end, n_used, n_slots)
        ys = _experts(blk_e, n_used, run_par, next_e, xs, w_up[l], bu, w_down[l], bd)
        y_p = _combine(dest_p, ys, gate_p, y1_p, mod, g_post_ffn[l], row0=0, tokens_per_row=tp)
        y_s = _combine(dest_s, ys, gate_s, y1_s, mod, g_post_ffn[l], row0=1, tokens_per_row=ss)

    return (y_p.reshape(bp, sp, D_MODEL), y_s.reshape(bs, ss, D_MODEL),
            jnp.stack(ks, axis=1), jnp.stack(vs, axis=1), jnp.stack(hs, axis=1))
```

```python
import functools
import math

import jax
import jax.numpy as jnp
from jax import lax
from jax.experimental import pallas as pl
from jax.experimental.pallas import tpu as pltpu

F32 = jnp.float32
BF16 = jnp.bfloat16

D_MODEL = 1024
N_HEADS = 8
HEAD_DIM = 64
V_DIM = 2 * HEAD_DIM
GRID_W = 64
D_RNN = D_MODEL
RNN_BLOCKS = 16
RNN_BLOCK_DIM = D_RNN // RNN_BLOCKS
CONV_W = 4
LRU_C = 8.0
N_EXPERTS = 32
TOP_K = 4
D_FF = D_MODEL
SWIGLU_ALPHA = 1.702
SWIGLU_LIMIT = 7.0
ROPE_THETA = 10000.0
MOE_BLOCK = 256
EPS = 1e-6
N_MOD = 6
IN_PARTS = 7

LANE = 128
SUBLANE = 8
ROW_TILE = 8
VMEM_LIMIT = 56 * 1024 * 1024

TM = 256
RNN_CB = 256
RNN_TT = 256
NEG_BIG = -1e30
CTX_HEADS_PER_STEP = 4
ONES_ROWS = 16
V_ROWS = V_DIM + ONES_ROWS
Q_SCALE =HEAD_DIM ** -0.5 * math.log2(math.e)


def _cparams(sem):
    return pltpu.CompilerParams(dimension_semantics=sem, vmem_limit_bytes=VMEM_LIMIT)


def _full(shape):
    return pl.BlockSpec(shape, lambda *_: (0,) * len(shape))


def _resident(shape):
    return pl.BlockSpec(shape, lambda *_: (0,) * len(shape), pipeline_mode=pl.Buffered(1))


def _rms(x, g):
    return x * lax.rsqrt(jnp.mean(x * x, axis=-1, keepdims=True) + EPS) * g


def _ada_kernel(c_ref, w_ref, b_ref, o_ref):
    c = c_ref[...]
    s = c * jax.nn.sigmoid(c)
    o_ref[...] = jnp.dot(s.astype(BF16), w_ref[...].astype(BF16), preferred_element_type=F32) + b_ref[...]


def _adaln(cvec, w_ada, b_ada):
    return pl.pallas_call(
        _ada_kernel,
        grid=(N_MOD,),
        in_specs=[_full((SUBLANE, D_MODEL)),
                  pl.BlockSpec((D_MODEL, D_MODEL), lambda j: (0, j)),
                  pl.BlockSpec((1, D_MODEL), lambda j: (0, j))],
        out_specs=pl.BlockSpec((SUBLANE, D_MODEL), lambda j: (0, j)),
        out_shape=jax.ShapeDtypeStruct((SUBLANE, N_MOD * D_MODEL), F32),
        compiler_params=_cparams(("arbitrary",)),
        name="adaln",
    )(cvec, w_ada, b_ada.reshape(1, -1))


def _mod_rows(mod_ref, row, part):
    return mod_ref[pl.ds(row, 1), part * D_MODEL:(part + 1) * D_MODEL]


def _inproj_kernel(*refs, rope, row0, tiles_per_row):
    if rope:
        (x_ref, mod_ref, g_ref, w_ref, cos_ref, sin_ref,
         q_ref, k_ref, v_ref, xr_ref, xg_ref, ga_ref, gr_ref) = refs
    else:
        (x_ref, mod_ref, g_ref, w_ref,
         q_ref, k_ref, v_ref, xr_ref, xg_ref, ga_ref, gr_ref, k32_ref, v32_ref) = refs
    row = row0 + pl.program_id(0) // tiles_per_row
    shift = _mod_rows(mod_ref, row, 0)
    scale = _mod_rows(mod_ref, row, 1)
    h = (_rms(x_ref[...], g_ref[...]) * (1.0 + scale) + shift).astype(BF16)

    def proj(j):
        return jnp.dot(h, w_ref[:, j * D_MODEL:(j + 1) * D_MODEL], preferred_element_type=F32)

    def rotate(x):
        lane = lax.broadcasted_iota(jnp.int32, (x.shape[0], LANE), 1)
        first = (lane & 31) < 16
        outs = []
        for c in range(D_MODEL // LANE):
            xc = x[:, c * LANE:(c + 1) * LANE]
            partner = jnp.where(first, pltpu.roll(xc, LANE - 16, 1), pltpu.roll(xc, 16, 1))
            outs.append(xc * cos_ref[...] + partner * sin_ref[...])
        return jnp.concatenate(outs, axis=1)

    q = proj(0)
    k = proj(1)
    v = proj(2)
    if rope:
        q = rotate(q)
        k = rotate(k)
    else:
        k32_ref[...] = k
        v32_ref[...] = v
    k_ref[...] = k.astype(BF16)
    q = q * Q_SCALE
    for hd in range(N_HEADS):
        cols = slice(hd * V_DIM, (hd + 1) * V_DIM)
        q_ref[0, hd] = q[:, cols].T.astype(BF16)
        v_ref[0, hd, 0, :V_DIM, :] = v[:, cols].T.astype(BF16)
        v_ref[0, hd, 0, V_DIM:, :] = jnp.ones((ONES_ROWS, v.shape[0]), BF16)
    xr_ref[...] = proj(3)
    xg_ref[...] = proj(4)
    ga_ref[...] = proj(5)
    gr_ref[...] = proj(6)


def _inproj(x, mod, g_pre, w_in_bf, *, row0, tokens_per_row, tokens_per_batch, rope_tabs=None):
    t = x.shape[0]
    rope = rope_tabs is not None
    tile = pl.BlockSpec((TM, D_MODEL), lambda i: (i, 0))
    in_specs = [tile, _full(mod.shape), _full((1, D_MODEL)), _resident(w_in_bf.shape)]
    args = [x, mod, g_pre.reshape(1, -1), w_in_bf]
    n_b = t // tokens_per_batch
    tpb = tokens_per_batch // TM
    outs = [jax.ShapeDtypeStruct((n_b, N_HEADS, V_DIM, tokens_per_batch), BF16),
            jax.ShapeDtypeStruct((t, D_MODEL), BF16),
            jax.ShapeDtypeStruct((n_b, N_HEADS, tpb, V_ROWS, TM), BF16)] + [jax.ShapeDtypeStruct((t, D_MODEL), F32)] * 4
    out_specs = [pl.BlockSpec((1, N_HEADS, V_DIM, TM), lambda i: (i // tpb, 0, 0, i % tpb)), tile,
                 pl.BlockSpec((1, N_HEADS, 1, V_ROWS, TM), lambda i: (i // tpb, 0, i % tpb, 0, 0))] + [tile] * 4
    if rope:
        n_pos = rope_tabs[0].shape[0] // TM
        tab = pl.BlockSpec((TM, LANE), lambda i: (i % n_pos, 0))
        in_specs += [tab, tab]
        args += list(rope_tabs)
    else:
        outs += [jax.ShapeDtypeStruct((t, D_MODEL), F32)] * 2
        out_specs += [tile] * 2
    return pl.pallas_call(
        functools.partial(_inproj_kernel, rope=rope, row0=row0, tiles_per_row=tokens_per_row // TM),
        grid=(t // TM,),
        in_specs=in_specs,
        out_specs=out_specs,
        out_shape=outs,
        compiler_params=_cparams(("arbitrary",)),
        name="inproj_lat" if rope else "inproj_ctx",
    )(*args)


def _rope_tables(n_tokens):
    rows = n_tokens // GRID_W
    row = jnp.repeat(jnp.arange(rows, dtype=F32), GRID_W)
    col = jnp.tile(jnp.arange(GRID_W, dtype=F32), rows)
    q4 = HEAD_DIM // 4
    inv_freq = jnp.power(ROPE_THETA, -jnp.arange(q4, dtype=F32) / q4)
    ang = jnp.concatenate([row[:, None] * inv_freq, col[:, None] * inv_freq], axis=-1)
    cos, sin = jnp.cos(ang), jnp.sin(ang)
    c64 = jnp.concatenate([cos[:, :q4], cos[:, :q4], cos[:, q4:], cos[:, q4:]], axis=-1)
    s64 = jnp.concatenate([-sin[:, :q4], sin[:, :q4], -sin[:, q4:], sin[:, q4:]], axis=-1)
    return jnp.tile(c64, (1, LANE // HEAD_DIM)), jnp.tile(s64, (1, LANE // HEAD_DIM))


def _attn_kernel(*refs, tk, n_new, n_cache, heads, lam_init):
    if n_cache:
        lq_ref, lk_ref, gs_ref, q_ref, k_ref, v_ref, kc_ref, vc_ref, o_ref, qs_ref, m_ref, acc_ref = refs
    else:
        lq_ref, lk_ref, gs_ref, q_ref, k_ref, v_ref, o_ref, qs_ref, m_ref, acc_ref = refs
    n_chunks = n_new + n_cache
    tq = q_ref.shape[3]
    sub = lax.broadcasted_iota(jnp.int32, (V_DIM, tq), 0)
    for h in range(heads):
        qt = q_ref[0, h]
        zero = jnp.zeros_like(qt)
        qs_ref[h, :, :tq] = jnp.where(sub < HEAD_DIM, qt, zero)
        qs_ref[h, :, tq:] = jnp.where(sub >= HEAD_DIM, qt, zero)
    m_ref[...] = jnp.full(m_ref.shape, -jnp.inf, F32)
    acc_ref[...] = jnp.zeros(acc_ref.shape, F32)

    def scores(item):
        c, h = item
        lanes = slice(h * LANE, (h + 1) * LANE)
        if c < n_new:
            kc = k_ref[0, c * tk:(c + 1) * tk, lanes]
        else:
            kc = kc_ref[0, (c - n_new) * tk:(c - n_new + 1) * tk, lanes]
        return jnp.dot(kc, qs_ref[h], preferred_element_type=F32)

    def values(c, h):
        return v_ref[0, h, c] if c < n_new else vc_ref[0, h, c - n_new]

    def accumulate(c, h, p, alpha):
        acc_ref[h] = alpha * acc_ref[h] + jnp.dot(values(c, h), p, preferred_element_type=F32)

    items = [(c, h) for c in range(n_chunks) for h in range(heads)]
    st = scores(items[0])
    pending = None
    for n, (c, h) in enumerate(items):
        st_next = scores(items[n + 1]) if n + 1 < len(items) else None
        if pending is not None:
            accumulate(*pending)
        m_old = m_ref[h]
        m_new = jnp.maximum(m_old, jnp.max(st, axis=0, keepdims=True))
        pending = (c, h, jnp.exp2(st - m_new).astype(BF16), jnp.exp2(m_old - m_new))
        m_ref[h] = m_new
        st = st_next
    accumulate(*pending)

    e = jnp.exp(jnp.sum(lq_ref[...] * lk_ref[...], axis=-1, keepdims=True))
    lam = e[0:1, :] - e[1:2, :] + lam_init
    for h in range(heads):
        acc = acc_ref[h]
        den = acc[V_DIM:V_DIM + 1, :]
        ot = acc[:V_DIM, :tq] / den[:, :tq] - lam * (acc[:V_DIM, tq:] / den[:, tq:])
        o_ref[0, :, h * LANE:(h + 1) * LANE] = (_rms(ot.T, gs_ref[...]) * (1.0 - lam_init)).astype(o_ref.dtype)


def _cache_layouts(cache_k, cache_v, tk):
    b, p = cache_k.shape[:2]
    kc = cache_k.reshape(b, p, D_MODEL).astype(BF16)
    vt = cache_v.transpose(0, 2, 3, 1)
    vt = jnp.concatenate([vt, jnp.ones((b, N_HEADS, ONES_ROWS, p), vt.dtype)], axis=2)
    return kc, vt.reshape(b, N_HEADS, V_ROWS, p // tk, tk).transpose(0, 1, 3, 2, 4).astype(BF16)


def _attention(qt, k, vt, lam_q, lam_k, g_sub, *, lam_init, tq, heads, cache=None):
    b, _, _, nq = qt.shape
    n = k.shape[1]
    tk = vt.shape[-1]
    n_new = n // tk
    in_specs = [_full((2, HEAD_DIM)), _full((2, HEAD_DIM)), _full((1, V_DIM)),
                pl.BlockSpec((1, heads, V_DIM, tq), lambda bi, h, qi: (bi, h, 0, qi)),
                pl.BlockSpec((1, n, heads * LANE), lambda bi, h, qi: (bi, 0, h)),
                pl.BlockSpec((1, heads, n_new, V_ROWS, tk), lambda bi, h, qi: (bi, h, 0, 0, 0))]
    args = [lam_q, lam_k, g_sub.reshape(1, -1), qt, k, vt]
    n_cache = 0
    if cache is not None:
        kc, vc = cache
        n_cache = vc.shape[2]
        in_specs += [pl.BlockSpec((1, kc.shape[1], heads * LANE), lambda bi, h, qi: (bi, 0, h)),
                     pl.BlockSpec((1, heads, n_cache, V_ROWS, tk), lambda bi, h, qi: (bi, h, 0, 0, 0))]
        args += [kc, vc]
    kern = functools.partial(_attn_kernel, tk=tk, n_new=n_new, n_cache=n_cache, heads=heads, lam_init=lam_init)
    return pl.pallas_call(
        kern,
        grid=(b, N_HEADS // heads, nq // tq),
        in_specs=in_specs,
        out_specs=pl.BlockSpec((1, tq, heads * LANE), lambda bi, h, qi: (bi, qi, h)),
        out_shape=jax.ShapeDtypeStruct((b, nq, D_MODEL), BF16),
        scratch_shapes=[pltpu.VMEM((heads, V_DIM, 2 * tq), BF16), pltpu.VMEM((heads, 1, 2 * tq), F32),
                        pltpu.VMEM((heads, V_ROWS, 2 * tq), F32)],
        compiler_params=_cparams(("arbitrary", "arbitrary", "arbitrary")),
        name=f"diff_attn_{n + n_cache * tk}",
    )(*args)


def _sigmoid(x):
    return 0.5 * jnp.tanh(0.5 * x) + 0.5


def _gelu_tanh(x):
    return 0.5 * x * (1.0 + jnp.tanh(math.sqrt(2.0 / math.pi) * (x + 0.044715 * (x * x * x))))


def _rnn_kernel(xr_ref, xg_ref, h0_ref, wc_ref, bc_ref, wg_ref, bg_ref, lam_ref,
                o_ref, hT_ref, xpad_ref, hf_ref, hb_ref, *, n, tt):
    pad = SUBLANE
    cb = xr_ref.shape[2]
    n_chunks = n // tt
    groups = tt // SUBLANE
    xpad_ref[0:pad, :] = jnp.zeros((pad, cb), F32)
    xpad_ref[pad + n:pad + n + pad, :] = jnp.zeros((pad, cb), F32)
    xpad_ref[pad:pad + n, :] = xr_ref[0]
    r8 = lax.broadcasted_iota(jnp.int32, (groups, SUBLANE, cb), 1)

    def gate_inputs(c, d):
        t0 = pl.multiple_of(c * tt, tt)
        blk = xpad_ref[pl.ds(t0, tt + 2 * pad), :]
        ext = tt + 2 * pad
        xc = (bc_ref[...]
              + wc_ref[0:1, :] * pltpu.roll(blk, 2, 0)[pad:pad + tt]
              + wc_ref[1:2, :] * pltpu.roll(blk, 1, 0)[pad:pad + tt]
              + wc_ref[2:3, :] * blk[pad:pad + tt]
              + wc_ref[3:4, :] * pltpu.roll(blk, ext - 1, 0)[pad:pad + tt])
        g = jnp.dot(xc.astype(BF16), wg_ref[d, 0], preferred_element_type=F32)
        r = _sigmoid(g[:, :cb] + bg_ref[d, 0:1, :])
        i = _sigmoid(g[:, cb:] + bg_ref[d, 1:2, :])
        lam = lam_ref[d:d + 1, :]
        softplus = jnp.maximum(-lam, 0.0) + jnp.log1p(jnp.exp(-jnp.abs(lam)))
        log_a = (-LRU_C) * r * softplus
        a = jnp.exp(log_a)
        u = jnp.sqrt(-jnp.tanh(log_a) * (1.0 + a * a)) * (i * xc)
        return t0, a, u

    def scan8(a, u, reverse):
        a = a.reshape(groups, SUBLANE, cb)
        u = u.reshape(groups, SUBLANE, cb)
        for d in (1, 2, 4):
            keep = (r8 < SUBLANE - d) if reverse else (r8 >= d)
            shift = SUBLANE - d if reverse else d
            u = u + jnp.where(keep, a * pltpu.roll(u, shift, 1), 0.0)
            a = jnp.where(keep, a * pltpu.roll(a, shift, 1), a)
        return a, u

    def fwd_chunk(c, hc):
        t0, a, u = gate_inputs(c, 0)
        a, u = scan8(a, u, False)
        for g in range(groups):
            hg = a[g] * hc + u[g]
            hf_ref[pl.ds(t0 + g * SUBLANE, SUBLANE), :] = hg
            hc = hg[SUBLANE - 1:SUBLANE]
        return hc

    def bwd_chunk(j, hc):
        c = n_chunks - 1 - j
        t0, a, u = gate_inputs(c, 1)
        a, u = scan8(a, u, True)
        for g in reversed(range(groups)):
            hg = a[g] * hc + u[g]
            hb_ref[g * SUBLANE:(g + 1) * SUBLANE, :] = hg
            hc = hg[0:1]
        hsum = hf_ref[pl.ds(t0, tt), :] + hb_ref[...]
        o_ref[0, pl.ds(t0, tt), :] = (hsum * _gelu_tanh(xg_ref[0, pl.ds(t0, tt), :])).astype(o_ref.dtype)
        return hc

    h_fwd = lax.fori_loop(0, n_chunks, fwd_chunk, h0_ref[0, 0:1, :])
    h_bwd = lax.fori_loop(0, n_chunks, bwd_chunk, h0_ref[0, 1:2, :])
    hT_ref[0, 0:1, :] = h_fwd
    hT_ref[0, 1:2, :] = h_bwd


def _rglru(xr, xg, h0, w_conv, b_conv, wg_bd, b_gate, lam):
    b, n, _ = xr.shape
    tt = min(RNN_TT, n)
    n_cb = D_RNN // RNN_CB
    slab = pl.BlockSpec((1, n, RNN_CB), lambda bi, ci: (bi, 0, ci))
    state = pl.BlockSpec((1, 2, RNN_CB), lambda bi, ci: (bi, 0, ci))
    return pl.pallas_call(
        functools.partial(_rnn_kernel, n=n, tt=tt),
        grid=(b, n_cb),
        in_specs=[slab, slab, state,
                  pl.BlockSpec((CONV_W, RNN_CB), lambda bi, ci: (0, ci)),
                  pl.BlockSpec((1, RNN_CB), lambda bi, ci: (0, ci)),
                  pl.BlockSpec((2, 1, RNN_CB, 2 * RNN_CB), lambda bi, ci: (0, ci, 0, 0)),
                  pl.BlockSpec((2, 2, RNN_CB), lambda bi, ci: (0, 0, ci)),
                  pl.BlockSpec((2, RNN_CB), lambda bi, ci: (0, ci))],
        out_specs=[slab, state],
        out_shape=[jax.ShapeDtypeStruct((b, n, D_RNN), BF16), jax.ShapeDtypeStruct((b, 2, D_RNN), F32)],
        scratch_shapes=[pltpu.VMEM((n + 2 * SUBLANE, RNN_CB), F32), pltpu.VMEM((n, RNN_CB), F32),
                        pltpu.VMEM((tt, RNN_CB), F32)],
        compiler_params=_cparams(("arbitrary", "arbitrary")),
        name=f"rglru_{n}",
    )(xr, xg, h0, w_conv, b_conv.reshape(1, -1), wg_bd, b_gate, lam)


def _gate_weights(w_gate):
    per = RNN_CB // RNN_BLOCK_DIM
    n_cb = D_RNN // RNN_CB
    w = w_gate.reshape(2, 2, n_cb, per, RNN_BLOCK_DIM, RNN_BLOCK_DIM)
    eye = jnp.eye(per, dtype=w.dtype)
    bd = jnp.einsum('dkcpij,pq->dkcpiqj', w, eye).reshape(2, 2, n_cb, RNN_CB, RNN_CB)
    return jnp.concatenate([bd[:, 0], bd[:, 1]], axis=-1).astype(BF16)


def _merge_kernel(oa_ref, or_ref, ga_ref, gr_ref, x_ref, mod_ref, base_ref, gpost_ref, gpre_ref,
                  wa_ref, wr_ref, wo_ref, wrh_ref, wrl_ref, br_ref,
                  y_ref, h_ref, info_ref, gate_ref, cnt_ref, run_ref, *, row0, tiles_per_row):
    i = pl.program_id(0)
    row = row0 + i // tiles_per_row
    ya = jnp.dot(oa_ref[...], wa_ref[...], preferred_element_type=F32)
    yr = jnp.dot(or_ref[...], wr_ref[...], preferred_element_type=F32)
    y = _sigmoid(ga_ref[...]) * ya + _sigmoid(gr_ref[...]) * yr
    mix = jnp.dot(y.astype(BF16), wo_ref[...], preferred_element_type=F32)
    y1 = x_ref[...] + _mod_rows(mod_ref, row, 2) * _rms(mix, gpost_ref[...])
    y_ref[...] = y1
    h = _rms(y1, gpre_ref[...]) * (1.0 + _mod_rows(mod_ref, row, 4)) + _mod_rows(mod_ref, row, 3)
    for j in range(ROW_TILE):
        h_ref[:, j, :] = h[:, j * LANE:(j + 1) * LANE]

    h_hi = h.astype(BF16)
    h_lo = (h - h_hi.astype(F32)).astype(BF16)
    logits = (jnp.dot(h_hi, wrh_ref[...], preferred_element_type=F32)
              + jnp.dot(h_lo, wrh_ref[...], preferred_element_type=F32)
              + jnp.dot(h_hi, wrl_ref[...], preferred_element_type=F32)) + br_ref[...]
    tm = logits.shape[0]
    lane = lax.broadcasted_iota(jnp.int32, (tm, LANE), 1)
    work = logits
    sel = []
    vals = []
    for _ in range(TOP_K):
        mx = jnp.max(work, axis=-1, keepdims=True)
        idx = jnp.min(jnp.where(work == mx, lane, LANE), axis=-1, keepdims=True)
        hit = lane == idx
        sel.append((idx, hit))
        vals.append(mx)
        work = jnp.where(hit, NEG_BIG * 2.0, work)
    exps = [jnp.exp(v - vals[0]) for v in vals]
    inv = 1.0 / (exps[0] + exps[1] + exps[2] + exps[3])

    @pl.when(i == 0)
    def _():
        run_ref[...] = base_ref[...]

    mask = jnp.zeros((tm, LANE), F32)
    for _, hit in sel:
        mask = mask + hit.astype(F32)
    rr = lax.broadcasted_iota(jnp.int32, (tm, tm), 0)
    cc = lax.broadcasted_iota(jnp.int32, (tm, tm), 1)
    tri = (cc < rr).astype(BF16)
    rank = jnp.dot(tri, mask.astype(BF16), preferred_element_type=F32) + run_ref[...]
    run_ref[...] = run_ref[...] + jnp.sum(mask, axis=0, keepdims=True)
    cnt_ref[...] = run_ref[...]

    info = jnp.zeros((tm, LANE), jnp.int32)
    gates = jnp.zeros((tm, LANE), F32)
    for kk, (idx, hit) in enumerate(sel):
        rk = jnp.sum(jnp.where(hit, rank, 0.0), axis=-1, keepdims=True).astype(jnp.int32)
        info = jnp.where(lane == kk, idx, info)
        info = jnp.where(lane == TOP_K + kk, rk, info)
        gates = jnp.where(lane == kk, exps[kk] * inv, gates)
    info_ref[...] = info
    gate_ref[...] = gates


def _merge(oa, orec, ga, gr, x, mod, base, g_post, g_pre, wa, wr, wo, wr_hi, wr_lo, b_router,
           *, row0, tokens_per_row):
    t = x.shape[0]
    tile = pl.BlockSpec((TM, D_MODEL), lambda i: (i, 0))
    small = pl.BlockSpec((TM, LANE), lambda i: (i, 0))
    vec = _full((1, D_MODEL))
    wspec = _resident((D_MODEL, D_MODEL))
    rspec = _resident((D_MODEL, LANE))
    return pl.pallas_call(
        functools.partial(_merge_kernel, row0=row0, tiles_per_row=tokens_per_row // TM),
        grid=(t // TM,),
        in_specs=[tile, tile, tile, tile, tile, _full(mod.shape), _full((1, LANE)), vec, vec,
                  wspec, wspec, wspec, rspec, rspec, _full((1, LANE))],
        out_specs=[tile, pl.BlockSpec((TM, ROW_TILE, LANE), lambda i: (i, 0, 0)), small, small, _full((1, LANE))],
        out_shape=[jax.ShapeDtypeStruct((t, D_MODEL), F32), jax.ShapeDtypeStruct((t, ROW_TILE, LANE), F32),
                   jax.ShapeDtypeStruct((t, LANE), jnp.int32), jax.ShapeDtypeStruct((t, LANE), F32),
                   jax.ShapeDtypeStruct((1, LANE), F32)],
        scratch_shapes=[pltpu.VMEM((1, LANE), F32)],
        compiler_params=_cparams(("arbitrary",)),
        name=f"merge_router_{t}",
    )(oa, orec, ga, gr, x, mod, base, g_post.reshape(1, -1), g_pre.reshape(1, -1),
      wa, wr, wo, wr_hi, wr_lo, b_router)


TD = 512


def _dispatch_kernel(dest_ref, ps_ref, pe_ref, nb_ref, hp_ref, hs_ref, xs_ref, zero_ref, sem, zsem,
                     *, n_p_tiles, n_blocks):
    i = pl.program_id(0)

    @pl.when(i == 0)
    def _():
        zero_ref[...] = jnp.zeros(zero_ref.shape, zero_ref.dtype)

        def zero_copy(start):
            return pltpu.make_async_copy(zero_ref, xs_ref.at[pl.ds(pl.multiple_of(start, MOE_BLOCK), MOE_BLOCK)], zsem)

        for act in ("start", "wait"):
            for e in range(N_EXPERTS):
                @pl.when(pe_ref[e] > ps_ref[e])
                def _():
                    getattr(zero_copy(pe_ref[e] - MOE_BLOCK), act)()

                @pl.when(n_blocks - 1 - e >= nb_ref[0])
                def _():
                    getattr(zero_copy((n_blocks - 1 - e) * MOE_BLOCK), act)()

    def scatter(h_ref):
        def copy(r, k):
            return pltpu.make_async_copy(h_ref.at[r], xs_ref.at[dest_ref[r * TOP_K + k]], sem)

        def start(r, c):
            for k in range(TOP_K):
                copy(r, k).start()
            return c

        def wait(r, c):
            for k in range(TOP_K):
                copy(r, k).wait()
            return c

        lax.fori_loop(0, TD, start, 0)
        lax.fori_loop(0, TD, wait, 0)

    @pl.when(i < n_p_tiles)
    def _():
        scatter(hp_ref)

    @pl.when(i >= n_p_tiles)
    def _():
        scatter(hs_ref)


def _dispatch(dest_flat, h_p, h_s, pad_start, pad_end, n_used, n_slots):
    n_p_tiles = h_p.shape[0] // TD
    n_tiles = n_p_tiles + h_s.shape[0] // TD
    n_blocks = n_slots // MOE_BLOCK
    assert n_blocks - N_EXPERTS >= 0
    smem = pl.BlockSpec(memory_space=pltpu.SMEM)
    return pl.pallas_call(
        functools.partial(_dispatch_kernel, n_p_tiles=n_p_tiles, n_blocks=n_blocks),
        grid=(n_tiles,),
        in_specs=[pl.BlockSpec((TD * TOP_K,), lambda i: (i,), memory_space=pltpu.SMEM), smem, smem, smem,
                  pl.BlockSpec((TD, ROW_TILE, LANE), lambda i: (jnp.minimum(i, n_p_tiles - 1), 0, 0)),
                  pl.BlockSpec((TD, ROW_TILE, LANE), lambda i: (jnp.maximum(i - n_p_tiles, 0), 0, 0))],
        out_specs=pl.BlockSpec(memory_space=pl.ANY),
        out_shape=jax.ShapeDtypeStruct((n_slots, ROW_TILE, LANE), h_p.dtype),
        scratch_shapes=[pltpu.VMEM((MOE_BLOCK, ROW_TILE, LANE), h_p.dtype), pltpu.SemaphoreType.DMA,
                        pltpu.SemaphoreType.DMA],
        compiler_params=_cparams(("arbitrary",)),
        name="moe_dispatch",
    )(dest_flat, pad_start, pad_end, n_used, h_p, h_s)


SEL_W = 256


def _expert_kernel(be_ref, nb_ref, par_ref, nxt_ref, xs_ref, wu_ref, bu_ref, wd_ref, bd_ref,
                   ys_ref, wuf_ref, wdf_ref, wub_ref, wdb_ref, wsem):
    i = pl.program_id(0)
    nb = nb_ref[0]
    e = be_ref[i]
    par = par_ref[i]

    def weight_copies(expert, p):
        return (pltpu.make_async_copy(wu_ref.at[expert], wuf_ref.at[p], wsem.at[0, p]),
                pltpu.make_async_copy(wd_ref.at[expert], wdf_ref.at[p], wsem.at[1, p]))

    @pl.when(i == 0)
    def _():
        for cp in weight_copies(e, par):
            cp.start()

    new_expert = (i == 0) | (e != be_ref[jnp.maximum(i - 1, 0)])

    @pl.when((i < nb) & new_expert)
    def _():
        for cp in weight_copies(e, par):
            cp.wait()

        @pl.when(nxt_ref[i] != e)
        def _():
            for cp in weight_copies(nxt_ref[i], 1 - par):
                cp.start()

        r = lax.broadcasted_iota(jnp.int32, (SEL_W, SEL_W), 0)
        c = lax.broadcasted_iota(jnp.int32, (SEL_W, SEL_W), 1)
        half = SEL_W // 2
        sel = jnp.where(r == jnp.where(c < half, 2 * c, 2 * (c - half) + 1), 1.0, 0.0).astype(BF16)
        for g in range(2 * D_FF // SEL_W):
            cols = wuf_ref[par, :, g * SEL_W:(g + 1) * SEL_W].astype(BF16)
            d = jnp.dot(cols, sel, preferred_element_type=F32).astype(BF16)
            wub_ref[:, g * half:(g + 1) * half] = d[:, :half]
            wub_ref[:, D_FF + g * half:D_FF + (g + 1) * half] = d[:, half:]
        wdb_ref[...] = wdf_ref[par].astype(BF16)

    @pl.when(i < nb)
    def _():
        x = jnp.concatenate([xs_ref[:, j, :] for j in range(ROW_TILE)], axis=1)
        hu = jnp.dot(x.astype(BF16), wub_ref[...], preferred_element_type=F32) + bu_ref[0]
        glu = jnp.minimum(hu[:, :D_FF], SWIGLU_LIMIT)
        lin = jnp.clip(hu[:, D_FF:], -SWIGLU_LIMIT, SWIGLU_LIMIT)
        act = (lin + 1.0) * glu * _sigmoid(SWIGLU_ALPHA * glu)
        ys_ref[...] = jnp.dot(act.astype(BF16), wdb_ref[...], preferred_element_type=F32) + bd_ref[0]

    @pl.when(i >= nb)
    def _():
        ys_ref[...] = jnp.zeros(ys_ref.shape, F32)


def _experts(blk_e, n_used, run_par, next_e, xs, wu, bu, wd, bd):
    ns = xs.shape[0]
    hbm = pl.BlockSpec(memory_space=pl.ANY)
    grid_spec = pltpu.PrefetchScalarGridSpec(
        num_scalar_prefetch=4,
        grid=(ns // MOE_BLOCK,),
        in_specs=[pl.BlockSpec((MOE_BLOCK, ROW_TILE, LANE), lambda i, be, nb, pr, nx: (jnp.minimum(i, nb[0] - 1), 0, 0)),
                  hbm,
                  pl.BlockSpec((1, 1, 2 * D_FF), lambda i, be, nb, pr, nx: (be[i], 0, 0)),
                  hbm,
                  pl.BlockSpec((1, 1, D_MODEL), lambda i, be, nb, pr, nx: (be[i], 0, 0))],
        out_specs=pl.BlockSpec((MOE_BLOCK, D_MODEL), lambda i, be, nb, pr, nx: (i, 0)),
        scratch_shapes=[pltpu.VMEM((2, D_MODEL, 2 * D_FF), F32), pltpu.VMEM((2, D_FF, D_MODEL), F32),
                        pltpu.VMEM((D_MODEL, 2 * D_FF), BF16), pltpu.VMEM((D_FF, D_MODEL), BF16),
                        pltpu.SemaphoreType.DMA((2, 2))],
    )
    return pl.pallas_call(
        _expert_kernel,
        grid_spec=grid_spec,
        out_shape=jax.ShapeDtypeStruct((ns, D_MODEL), F32),
        compiler_params=_cparams(("arbitrary",)),
        name="moe_experts",
    )(blk_e, n_used, run_par, next_e, xs, wu, bu, wd, bd)


def _combine_kernel(dest_ref, ys_ref, gate_ref, y1_ref, mod_ref, gpost_ref, o_ref, buf_ref, sem,
                    *, row0, tiles_per_row):
    row = row0 + pl.program_id(0) // tiles_per_row

    def copy(r, k):
        return pltpu.make_async_copy(ys_ref.at[pl.ds(dest_ref[r * TOP_K + k], 1), :],
                                     buf_ref.at[k, pl.ds(r, 1), :], sem)

    def start(r, c):
        for k in range(TOP_K):
            copy(r, k).start()
        return c

    def wait(r, c):
        for k in range(TOP_K):
            copy(r, k).wait()
        return c

    lax.fori_loop(0, TD, start, 0)
    lax.fori_loop(0, TD, wait, 0)
    g = gate_ref[...]
    moe = g[:, 0:1] * buf_ref[0]
    for kk in range(1, TOP_K):
        moe = moe + g[:, kk:kk + 1] * buf_ref[kk]
    o_ref[...] = y1_ref[...] + _mod_rows(mod_ref, row, 5) * _rms(moe, gpost_ref[...])


def _combine(dest_flat, ys, gates, y1, mod, g_post, *, row0, tokens_per_row):
    t = y1.shape[0]
    tile = pl.BlockSpec((TD, D_MODEL), lambda i: (i, 0))
    return pl.pallas_call(
        functools.partial(_combine_kernel, row0=row0, tiles_per_row=tokens_per_row // TD),
        grid=(t // TD,),
        in_specs=[pl.BlockSpec((TD * TOP_K,), lambda i: (i,), memory_space=pltpu.SMEM),
                  pl.BlockSpec(memory_space=pl.ANY),
                  pl.BlockSpec((TD, LANE), lambda i: (i, 0)), tile, _full(mod.shape), _full((1, D_MODEL))],
        out_specs=tile,
        out_shape=jax.ShapeDtypeStruct((t, D_MODEL), F32),
        scratch_shapes=[pltpu.VMEM((TOP_K, TD, D_MODEL), F32), pltpu.SemaphoreType.DMA],
        compiler_params=_cparams(("arbitrary",)),
        name=f"moe_combine_{t}",
    )(dest_flat, ys, gates, y1, mod, g_post.reshape(1, -1))


def kernel(x_prompt, x_sample, cache_k, cache_v, state_h, c, c_ctx, w_ada, b_ada, g_pre_mix, g_post_mix, g_pre_ffn, g_post_ffn, w_in, lam_q, lam_k, g_subln, w_conv, b_conv, w_lru_gate, b_lru_gate, lru_lambda, w_attn_proj, w_rec_proj, w_out, w_router, b_router, w_up, b_up, w_down, b_down):
    depth = w_in.shape[0]
    bp, sp, _ = x_prompt.shape
    bs, ss, _ = x_sample.shape
    tp, ts = bp * sp, bs * ss
    cos_t, sin_t = _rope_tables(ss)

    y_p = x_prompt.reshape(tp, D_MODEL)
    y_s = x_sample.reshape(ts, D_MODEL)
    ks, vs, hs = [], [], []
    for l in range(depth):
        lam_init = 0.8 - 0.6 * math.exp(-0.3 * l)
        cvec = jnp.zeros((SUBLANE, D_MODEL), F32).at[0].set(c_ctx).at[1:1 + bs].set(c)
        mod = _adaln(cvec, w_ada[l], b_ada[l])
        w_in_bf = w_in[l].astype(BF16)
        wa, wr, wo = w_attn_proj[l].astype(BF16), w_rec_proj[l].astype(BF16), w_out[l].astype(BF16)
        wg_bd = _gate_weights(w_lru_gate[l])
        w_rt = jnp.pad(w_router[l], ((0, 0), (0, LANE - N_EXPERTS)))
        wr_hi = w_rt.astype(BF16)
        wr_lo = (w_rt - wr_hi.astype(F32)).astype(BF16)
        b_rt = jnp.pad(b_router[l], (0, LANE - N_EXPERTS), constant_values=NEG_BIG).reshape(1, LANE)
        bu = b_up[l].reshape(N_EXPERTS, D_FF, 2)
        bu = jnp.concatenate([bu[..., 0], bu[..., 1]], axis=-1).reshape(N_EXPERTS, 1, 2 * D_FF)
        bd = b_down[l].reshape(N_EXPERTS, 1, D_MODEL)

        qt, k, vt, xr, xg, ga, gr, k32, v32 = _inproj(y_p, mod, g_pre_mix[l], w_in_bf, row0=0, tokens_per_row=tp,
                                                      tokens_per_batch=sp)
        oa = _attention(qt, k.reshape(bp, sp, D_MODEL), vt, lam_q[l], lam_k[l], g_subln[l],
                        lam_init=lam_init, tq=sp, heads=CTX_HEADS_PER_STEP)
        orec, h_t = _rglru(xr.reshape(bp, sp, D_RNN), xg.reshape(bp, sp, D_RNN),
                           jnp.zeros((bp, 2, D_RNN), F32), w_conv[l], b_conv[l], wg_bd, b_lru_gate[l], lru_lambda[l])
        ks.append(k32.reshape(bp, sp, N_HEADS, 2, HEAD_DIM))
        vs.append(v32.reshape(bp, sp, N_HEADS, V_DIM))
        hs.append(h_t)
        y1_p, h2_p, info_p, gate_p, cnt_p = _merge(
            oa.reshape(tp, D_MODEL), orec.reshape(tp, D_MODEL), ga, gr, y_p, mod, jnp.zeros((1, LANE), F32),
            g_post_mix[l], g_pre_ffn[l], wa, wr, wo, wr_hi, wr_lo, b_rt, row0=0, tokens_per_row=tp)

        qt, k, vt, xr, xg, ga, gr = _inproj(y_s, mod, g_pre_mix[l], w_in_bf, row0=1, tokens_per_row=ss,
                                            tokens_per_batch=ss, rope_tabs=(cos_t, sin_t))
        oa = _attention(qt, k.reshape(bs, ss, D_MODEL), vt, lam_q[l], lam_k[l], g_subln[l],
                        lam_init=lam_init, tq=1024, heads=1, cache=_cache_layouts(cache_k[:, l], cache_v[:, l], TM))
        orec, _ = _rglru(xr.reshape(bs, ss, D_RNN), xg.reshape(bs, ss, D_RNN), state_h[:, l],
                         w_conv[l], b_conv[l], wg_bd, b_lru_gate[l], lru_lambda[l])
        y1_s, h2_s, info_s, gate_s, cnt_s = _merge(
            oa.reshape(ts, D_MODEL), orec.reshape(ts, D_MODEL), ga, gr, y_s, mod, cnt_p,
            g_post_mix[l], g_pre_ffn[l], wa, wr, wo, wr_hi, wr_lo, b_rt, row0=1, tokens_per_row=ss)

        counts = cnt_s[0, :N_EXPERTS].astype(jnp.int32)
        padded = (counts + MOE_BLOCK - 1) // MOE_BLOCK * MOE_BLOCK
        pad_end = jnp.cumsum(padded)
        pad_start = pad_end - padded
        n_slots = (tp + ts) * TOP_K + N_EXPERTS * MOE_BLOCK
        n_blocks = n_slots // MOE_BLOCK
        blk_start = jnp.arange(n_blocks, dtype=jnp.int32) * MOE_BLOCK
        blk_e = jnp.minimum(jnp.sum(pad_end[None, :] <= blk_start[:, None], axis=1), N_EXPERTS - 1).astype(jnp.int32)
        n_used = (pad_end[-1:] // MOE_BLOCK).astype(jnp.int32)

        def dest_of(info):
            e = info[:, :TOP_K]
            onehot = e[:, :, None] == jnp.arange(N_EXPERTS, dtype=jnp.int32)[None, None, :]
            return (jnp.sum(jnp.where(onehot, pad_start[None, None, :], 0), axis=-1)
                    + info[:, TOP_K:2 * TOP_K]).reshape(-1)

        dest_p, dest_s = dest_of(info_p), dest_of(info_s)
        run_par = (jnp.cumsum(jnp.concatenate([jnp.zeros((1,), jnp.int32),
                                               (blk_e[1:] != blk_e[:-1]).astype(jnp.int32)])) % 2).astype(jnp.int32)
        ids = jnp.arange(N_EXPERTS, dtype=jnp.int32)
        later = lax.cummin(jnp.where(padded > 0, ids, N_EXPERTS), reverse=True)
        next_active = jnp.concatenate([later[1:], jnp.full((1,), N_EXPERTS, jnp.int32)])
        next_active = jnp.where(next_active < N_EXPERTS, next_active, ids)
        next_e = jnp.sum(jnp.where(blk_e[:, None] == ids[None, :], next_active[None, :], 0), axis=1).astype(jnp.int32)
        xs = _dispatch(jnp.concatenate([dest_p, dest_s]), h2_p, h2_s, pad_start, pad_end, n_used, n_slots)
        ys = _experts(blk_e, n_used, run_par, next_e, xs, w_up[l], bu, w_down[l], bd)
        y_p = _combine(dest_p, ys, gate_p, y1_p, mod, g_post_ffn[l], row0=0, tokens_per_row=tp)
        y_s = _combine(dest_s, ys, gate_s, y1_s, mod, g_post_ffn[l], row0=1, tokens_per_row=ss)

    return (y_p.reshape(bp, sp, D_MODEL), y_s.reshape(bs, ss, D_MODEL),
            jnp.stack(ks, axis=1), jnp.stack(vs, axis=1), jnp.stack(hs, axis=1))
```
